```python
import jax, jax.numpy as jnp
from jax import lax
import numpy as np

D_MODEL = 1024
BATCH = 32
SEQ = 2048
DEPTH = 2

D_MIX = D_MODEL
GLA_HEADS = 4
GLA_HEAD_K = D_MIX // 16
GLA_HEAD_V = D_MIX // 8
GLA_GATE_RANK = 16
GLA_GATE_NORMALIZER = 16.0
HGRN_HEADS = 4
HGRN_HEAD_F = D_MIX // 16
HGRN_HEAD_I = D_MIX // 16
CONV_CHANNELS = D_MIX // 4
CONV_WIDTH = 3
CHUNK = 64
N_EXPERTS = 32
TOP_K = 4
D_FF = D_MODEL
SWIGLU_LIMIT = 7.0
SWIGLU_ALPHA = 1.702
MOE_BLOCK = 128
LN_EPS = 1e-5
RMS_EPS = 1e-6
DEEPNORM_ALPHA = (2 * DEPTH) ** 0.25
DEEPNORM_BETA = (8 * DEPTH) ** -0.25

GLA_K_DIM = GLA_HEADS * GLA_HEAD_K
GLA_V_DIM = GLA_HEADS * GLA_HEAD_V
HGRN_F_DIM = HGRN_HEADS * HGRN_HEAD_F
HGRN_I_DIM = HGRN_HEADS * HGRN_HEAD_I
IN_SPLITS = (GLA_K_DIM, GLA_K_DIM, GLA_V_DIM, GLA_GATE_RANK, GLA_V_DIM,
             HGRN_F_DIM, HGRN_F_DIM, HGRN_I_DIM, HGRN_I_DIM,
             CONV_CHANNELS, CONV_CHANNELS, CONV_CHANNELS)
D_IN_PROJ = sum(IN_SPLITS)

kernel_name = "hybrid_gla_hgrn2_shortconv_moe_deepnorm"


def layer_norm(x, g, b):
    xf = x.astype(jnp.float32)
    mu = jnp.mean(xf, axis=-1, keepdims=True)
    var = jnp.mean(jnp.square(xf - mu), axis=-1, keepdims=True)
    y = (xf - mu) * lax.rsqrt(var + LN_EPS) * g.astype(jnp.float32) + b.astype(jnp.float32)
    return y.astype(x.dtype)


def gated_head_rmsnorm(o, gate, g):
    bsz, t, h, dv = o.shape
    of = o.astype(jnp.float32)
    of = of * lax.rsqrt(jnp.mean(of * of, axis=-1, keepdims=True) + RMS_EPS) * g.astype(jnp.float32)
    return (of.reshape(bsz, t, h * dv) * jax.nn.silu(gate.astype(jnp.float32))).astype(gate.dtype)


def chunk_gated_linear_attention(q, k, v, log_a):
    bsz, t, h, dk = q.shape
    dv = v.shape[-1]
    n_chunks = t // CHUNK

    def to_chunks(z):
        return z.astype(jnp.float32).reshape(bsz, n_chunks, CHUNK, h, z.shape[-1]).transpose(1, 0, 3, 2, 4)

    causal = jnp.tril(jnp.ones((CHUNK, CHUNK), dtype=bool))[:, :, None]

    def step(state, blk):
        qb, kb, vb, gb = blk
        g_cum = jnp.cumsum(gb, axis=2)
        rel = g_cum[:, :, :, None, :] - g_cum[:, :, None, :, :]
        decay = jnp.where(causal, jnp.exp(jnp.minimum(rel, 0.0)), 0.0)
        scores = jnp.einsum("bhid,bhjd,bhijd->bhij", qb, kb, decay)
        o_intra = jnp.einsum("bhij,bhjv->bhiv", scores, vb)
        o_inter = jnp.einsum("bhid,bhdv->bhiv", qb * jnp.exp(g_cum), state)
        g_last = g_cum[:, :, -1:, :]
        new_state = (jnp.exp(g_last[:, :, 0, :, None]) * state
                     + jnp.einsum("bhjd,bhjv->bhdv", kb * jnp.exp(g_last - g_cum), vb))
        return new_state, o_intra + o_inter

    state0 = jnp.zeros((bsz, h, dk, dv), jnp.float32)
    _, o = lax.scan(step, state0, (to_chunks(q), to_chunks(k), to_chunks(v), to_chunks(log_a)))
    return o.transpose(1, 0, 3, 2, 4).reshape(bsz, t, h, dv).astype(v.dtype)


def hybrid_mixer(h, w_in, w_gla_gate_expand, b_gla_gate, gla_norm_g, lower_bound, hgrn_norm_g, conv_w, w_out):
    bsz, t, _ = h.shape
    proj = jnp.einsum("btd,de->bte", h, w_in)
    split_idx = np.cumsum(IN_SPLITS)[:-1].tolist()
    (gla_q, gla_k, gla_v, gla_lr, gla_og,
     hg_q, hg_f, hg_i, hg_og,
     cv_b, cv_c, cv_h) = jnp.split(proj, split_idx, axis=-1)

    q = gla_q.reshape(bsz, t, GLA_HEADS, GLA_HEAD_K) * (GLA_HEAD_K ** -0.5)
    k = gla_k.reshape(bsz, t, GLA_HEADS, GLA_HEAD_K)
    v = gla_v.reshape(bsz, t, GLA_HEADS, GLA_HEAD_V)
    gate_logits = jnp.einsum("btr,rk->btk", gla_lr, w_gla_gate_expand) + b_gla_gate
    log_a = (jax.nn.log_sigmoid(gate_logits.astype(jnp.float32)) / GLA_GATE_NORMALIZER)
    o_gla = chunk_gated_linear_attention(q, k, v, log_a.reshape(bsz, t, GLA_HEADS, GLA_HEAD_K))
    y_gla = gated_head_rmsnorm(o_gla, gla_og, gla_norm_g)

    z = hg_f.astype(jnp.float32)
    lb = lower_bound.astype(jnp.float32)
    f = lb + (1.0 - lb) * jax.nn.sigmoid(z)
    log_f = jnp.log(f)
    k_h = 1.0 - f
    q_h = jax.nn.silu(hg_q)
    o_h = chunk_gated_linear_attention(
        q_h.reshape(bsz, t, HGRN_HEADS, HGRN_HEAD_F),
        k_h.reshape(bsz, t, HGRN_HEADS, HGRN_HEAD_F),
        hg_i.reshape(bsz, t, HGRN_HEADS, HGRN_HEAD_I),
        log_f.reshape(bsz, t, HGRN_HEADS, HGRN_HEAD_F))
    y_hgrn = gated_head_rmsnorm(o_h, hg_og, hgrn_norm_g)

    u = cv_c * cv_h
    u_pad = jnp.pad(u, ((0, 0), (CONV_WIDTH - 1, 0), (0, 0)))
    conv = sum(conv_w[j] * u_pad[:, j:j + t, :] for j in range(CONV_WIDTH))
    y_conv = cv_b * conv

    y = jnp.concatenate([y_gla, y_hgrn, y_conv.astype(y_gla.dtype)], axis=-1)
    return jnp.einsum("bte,ed->btd", y, w_out)


def moe_sequence(xs, w_router, b_router, w_gate_up, b_gate_up, w_down, b_down):
    s = xs.shape[0]
    logits = jnp.einsum("sd,de->se", xs.astype(jnp.float32), w_router.astype(jnp.float32)) + b_router.astype(jnp.float32)
    top_logits, top_idx = lax.top_k(logits, TOP_K)
    gates = jax.nn.softmax(top_logits, axis=-1)
    m = s * TOP_K
    n_pad = m + N_EXPERTS * MOE_BLOCK
    n_blocks = n_pad // MOE_BLOCK
    expert_flat = top_idx.reshape(m)
    token_flat = jnp.arange(m, dtype=jnp.int32) // TOP_K
    order = jnp.argsort(expert_flat)
    expert_sorted = expert_flat[order]
    counts = jnp.zeros((N_EXPERTS,), jnp.int32).at[expert_flat].add(1)
    padded = (counts + MOE_BLOCK - 1) // MOE_BLOCK * MOE_BLOCK
    starts = jnp.cumsum(counts) - counts
    padded_ends = jnp.cumsum(padded)
    padded_starts = padded_ends - padded
    dest = padded_starts[expert_sorted] + jnp.arange(m, dtype=jnp.int32) - starts[expert_sorted]
    slot_token = jnp.zeros((n_pad,), jnp.int32).at[dest].set(token_flat[order])
    slot_gate = jnp.zeros((n_pad,), jnp.float32).at[dest].set(gates.reshape(m)[order])
    block_expert = jnp.minimum(
        jnp.searchsorted(padded_ends, jnp.arange(n_blocks, dtype=jnp.int32) * MOE_BLOCK, side="right"),
        N_EXPERTS - 1)
    x_slots = xs[slot_token].reshape(n_blocks, MOE_BLOCK, xs.shape[-1])

    def expert_block(args):
        xb, e = args
        gu = xb @ w_gate_up[e] + b_gate_up[e]
        gate, up = jnp.split(gu, 2, axis=-1)
        gate = jnp.minimum(gate, SWIGLU_LIMIT)
        up = jnp.clip(up, -SWIGLU_LIMIT, SWIGLU_LIMIT)
        act = (up + 1.0) * gate * jax.nn.sigmoid(SWIGLU_ALPHA * gate)
        return act @ w_down[e] + b_down[e]

    y_slots = lax.map(expert_block, (x_slots, block_expert)).reshape(n_pad, xs.shape[-1])
    weighted = y_slots * slot_gate[:, None].astype(y_slots.dtype)
    return jax.ops.segment_sum(weighted, slot_token, num_segments=s)


def setup_inputs(seed: int = 0) -> dict:
    key = jax.random.key(seed)
    ks = jax.random.split(key, 24)

    def nrm(k, shape, scale):
        return jax.random.normal(k, shape, jnp.float32) * scale

    L = DEPTH
    return {
        "x": nrm(ks[0], (BATCH, SEQ, D_MODEL), 1.0),
        "ln_in_g": 1.0 + nrm(ks[1], (D_MODEL,), 0.02),
        "ln_in_b": nrm(ks[2], (D_MODEL,), 0.02),
        "w_in": nrm(ks[3], (L, D_MODEL, D_IN_PROJ), D_MODEL ** -0.5),
        "w_gla_gate_expand": nrm(ks[4], (L, GLA_GATE_RANK, GLA_K_DIM), GLA_GATE_RANK ** -0.5),
        "b_gla_gate": nrm(ks[5], (L, GLA_K_DIM), 0.1),
        "gla_norm_g": 1.0 + nrm(ks[6], (L, GLA_HEAD_V), 0.02),
        "hgrn_lb_raw": nrm(ks[7], (L, HGRN_F_DIM), 0.5),
        "hgrn_norm_g": 1.0 + nrm(ks[8], (L, HGRN_HEAD_I), 0.02),
        "conv_w": nrm(ks[9], (L, CONV_WIDTH, CONV_CHANNELS), CONV_WIDTH ** -0.5),
        "w_out": nrm(ks[10], (L, D_MIX, D_MODEL), DEEPNORM_BETA * D_MIX ** -0.5),
        "ln_mix_g": 1.0 + nrm(ks[11], (L, D_MODEL), 0.02),
        "ln_mix_b": nrm(ks[12], (L, D_MODEL), 0.02),
        "w_router": nrm(ks[13], (L, D_MODEL, N_EXPERTS), D_MODEL ** -0.5),
        "b_router": nrm(ks[14], (L, N_EXPERTS), 0.01),
        "w_gate_up": nrm(ks[15], (L, N_EXPERTS, D_MODEL, 2 * D_FF), D_MODEL ** -0.5),
        "b_gate_up": nrm(ks[16], (L, N_EXPERTS, 2 * D_FF), 0.01),
        "w_down": nrm(ks[17], (L, N_EXPERTS, D_FF, D_MODEL), DEEPNORM_BETA * D_FF ** -0.5),
        "b_down": nrm(ks[18], (L, N_EXPERTS, D_MODEL), 0.01),
        "ln_ffn_g": 1.0 + nrm(ks[19], (L, D_MODEL), 0.02),
        "ln_ffn_b": nrm(ks[20], (L, D_MODEL), 0.02),
    }


def reference(x, ln_in_g, ln_in_b, w_in, w_gla_gate_expand, b_gla_gate, gla_norm_g, hgrn_lb_raw,
              hgrn_norm_g, conv_w, w_out, ln_mix_g, ln_mix_b, w_router, b_router, w_gate_up,
              b_gate_up, w_down, b_down, ln_ffn_g, ln_ffn_b):
    lb_p = jax.nn.softmax(hgrn_lb_raw.astype(jnp.float32), axis=0)
    lower_bounds = jnp.cumsum(lb_p, axis=0) - lb_p[0]

    h = layer_norm(x, ln_in_g, ln_in_b)
    for l in range(DEPTH):
        mix = hybrid_mixer(h, w_in[l], w_gla_gate_expand[l], b_gla_gate[l], gla_norm_g[l],
                           lower_bounds[l], hgrn_norm_g[l], conv_w[l], w_out[l])
        h = layer_norm(DEEPNORM_ALPHA * h + mix, ln_mix_g[l], ln_mix_b[l])
        ffn = lax.map(lambda xs: moe_sequence(xs, w_router[l], b_router[l], w_gate_up[l],
                                               b_gate_up[l], w_down[l], b_down[l]), h)
        h = layer_norm(DEEPNORM_ALPHA * h + ffn.astype(h.dtype), ln_ffn_g[l], ln_ffn_b[l])
    return h
```

```python
import functools

import jax
import jax.numpy as jnp
from jax import lax
from jax.experimental import pallas as pl
from jax.experimental.pallas import tpu as pltpu

F32 = jnp.float32
BF16 = jnp.bfloat16

GLA_HEADS = 4
GLA_HEAD_K = 64
GLA_HEAD_V = 128
GLA_GATE_RANK = 16
GLA_GATE_NORMALIZER = 16.0
HGRN_HEADS = 4
HGRN_HEAD_F = 64
HGRN_HEAD_I = 64
CONV_CHANNELS = 256
CONV_WIDTH = 3
CHUNK = 64
N_EXPERTS = 32
TOP_K = 4
SWIGLU_LIMIT = 7.0
SWIGLU_ALPHA = 1.702
LN_EPS = 1e-5
RMS_EPS = 1e-6

GLA_K_DIM = GLA_HEADS * GLA_HEAD_K
GLA_V_DIM = GLA_HEADS * GLA_HEAD_V
HGRN_F_DIM = HGRN_HEADS * HGRN_HEAD_F
HGRN_I_DIM = HGRN_HEADS * HGRN_HEAD_I
IN_SPLITS = (GLA_K_DIM, GLA_K_DIM, GLA_V_DIM, GLA_GATE_RANK, GLA_V_DIM,
             HGRN_F_DIM, HGRN_F_DIM, HGRN_I_DIM, HGRN_I_DIM,
             CONV_CHANNELS, CONV_CHANNELS, CONV_CHANNELS)

LANES = 128
SUBLANES = 8
VMEM_LIMIT_BYTES = 56 * 1024 * 1024

MIX_ROWS = 256
ROUTE_ROWS = 512
DISPATCH_ROWS = 512
EXPERT_ROWS = 512
COMBINE_ROWS = 256

_ORDER = (0, 1, 2, 4, 5, 6, 7, 8, 9, 10, 11, 3)
_SRC_OFF = [sum(IN_SPLITS[:i]) for i in range(len(IN_SPLITS))]
_COLS = {}
_off = 0
for _g in _ORDER:
    _w = IN_SPLITS[_g]
    _COLS[_g] = (_off, _off + _w)
    _off += -(-_w // LANES) * LANES
D_IN_PAD = _off
(C_GQ, C_GK, C_GV, C_GLR, C_GOG, C_HQ, C_HF, C_HI, C_HOG, C_CB, C_CC, C_CH) = (_COLS[i] for i in range(12))


def _layer_norm(x, g, b):
    mu = jnp.mean(x, axis=-1, keepdims=True)
    xc = x - mu
    var = jnp.mean(xc * xc, axis=-1, keepdims=True)
    return xc * lax.rsqrt(var + LN_EPS) * g + b


def _sigmoid(x):
    return 1.0 / (1.0 + jnp.exp(-x))


def _dot(a, b):
    return jnp.dot(a, b, preferred_element_type=F32)


def _dot_nt(a, b):
    return lax.dot_general(a, b, (((1,), (1,)), ((), ())), preferred_element_type=F32)


def _dot_tn(a, b):
    return lax.dot_general(a, b, (((0,), (0,)), ((), ())), preferred_element_type=F32)


def _split_dot(a01, x):
    hi = x.astype(BF16)
    lo = (x - hi.astype(F32)).astype(BF16)
    return _dot(a01, hi) + _dot(a01, lo)


def _columns_to_lanes(cols, width):
    lane = lax.broadcasted_iota(jnp.int32, (cols[0].shape[0], width), 1)
    out = jnp.broadcast_to(cols[-1], lane.shape)
    for k in range(width - 2, -1, -1):
        out = jnp.where(lane == k, cols[k], out)
    return out


def _chunk_recurrence(q, k, v, g, st_ref, tril, causal, bmask, heads, dk, dv):
    rows = q.shape[0]
    gc = _split_dot(tril, g)
    lane_k = lax.broadcasted_iota(jnp.int32, (1, heads * dk), 1) // dk
    lane_v = lax.broadcasted_iota(jnp.int32, (1, heads * dv), 1) // dv
    st = st_ref[...]
    outs = []
    for c in range(rows // CHUNK):
        sl = slice(c * CHUNK, (c + 1) * CHUNK)
        gcc, qc, kc, vc = gc[sl], q[sl], k[sl], v[sl]
        mid = gcc[CHUNK // 2 - 1:CHUNK // 2]
        last = gcc[CHUNK - 1:CHUNK]
        qt = qc * jnp.exp(gcc - mid)
        kt = kc * jnp.exp(mid - gcc)
        qi = qc * jnp.exp(gcc)
        ks = kc * jnp.exp(last - gcc)
        vb = vc.astype(BF16)
        qstack = jnp.concatenate([jnp.where(lane_k == h, qt, 0.0) for h in range(heads)], axis=0)
        a = _dot_nt(qstack.astype(BF16), kt.astype(BF16))
        a = jnp.where(causal, a, 0.0)
        o_all = _dot(a.astype(BF16), vb)
        o_intra = jnp.where(lane_v == 0, o_all[0:CHUNK], 0.0)
        for h in range(1, heads):
            o_intra = o_intra + jnp.where(lane_v == h, o_all[h * CHUNK:(h + 1) * CHUNK], 0.0)
        o_inter = _dot_nt(qi.astype(BF16), st.astype(BF16))
        u = _dot_tn(vb, ks.astype(BF16))
        st = jnp.exp(last) * st + u * bmask
        outs.append(o_intra + o_inter)
    st_ref[...] = st
    return jnp.concatenate(outs, axis=0)


def _gated_head_rmsnorm(o, gate, g_tiled, heads, dv):
    lane_v = lax.broadcasted_iota(jnp.int32, (1, heads * dv), 1) // dv
    sq = o * o
    ms = jnp.zeros_like(o)
    for h in range(heads):
        m = lane_v == h
        ms = jnp.where(m, jnp.sum(jnp.where(m, sq, 0.0), axis=-1, keepdims=True), ms)
    ms = ms * (1.0 / dv)
    return o * lax.rsqrt(ms + RMS_EPS) * g_tiled * (gate * _sigmoid(gate))


def _mixer_kernel(x_ref, lng_ref, lnb_ref, win_ref, wexp_ref, bgate_ref, glag_ref, lbraw_ref, hgg_ref,
                  convw_ref, wout_ref, mg_ref, mb_ref, tril_ref, causal_ref, bmg_ref, bmh_ref,
                  o_ref, sg_ref, sh_ref, cc_ref, *, layer, pre_ln, alpha):
    @pl.when(pl.program_id(1) == 0)
    def _():
        sg_ref[...] = jnp.zeros_like(sg_ref)
        sh_ref[...] = jnp.zeros_like(sh_ref)
        cc_ref[...] = jnp.zeros_like(cc_ref)

    x = x_ref[...]
    h = _layer_norm(x, lng_ref[...], lnb_ref[...]) if pre_ln else x
    hb = h.astype(BF16)
    rows = x.shape[0]

    def proj(cols):
        return _dot(hb, win_ref[:, cols[0]:cols[0] + -(-(cols[1] - cols[0]) // LANES) * LANES])

    tril = tril_ref[...]
    causal = causal_ref[...] > 0.5

    q = proj(C_GQ) * (GLA_HEAD_K ** -0.5)
    gate_logits = lax.dot_general(proj(C_GLR), wexp_ref[...], (((1,), (0,)), ((), ())),
                                  precision=lax.Precision.HIGHEST, preferred_element_type=F32) + bgate_ref[...]
    log_a = (jnp.minimum(gate_logits, 0.0) - jnp.log(1.0 + jnp.exp(-jnp.abs(gate_logits)))) * (1.0 / GLA_GATE_NORMALIZER)
    o_gla = _chunk_recurrence(q, proj(C_GK), proj(C_GV), log_a, sg_ref, tril, causal, bmg_ref[...],
                              GLA_HEADS, GLA_HEAD_K, GLA_HEAD_V)
    y_gla = _gated_head_rmsnorm(o_gla, proj(C_GOG), glag_ref[...], GLA_HEADS, GLA_HEAD_V)

    raw = lbraw_ref[...]
    e = jnp.exp(raw - jnp.max(raw, axis=0, keepdims=True))
    lb = jnp.zeros_like(e[0:1])
    for i in range(1, layer + 1):
        lb = lb + e[i:i + 1]
    lb = lb / jnp.sum(e, axis=0, keepdims=True)
    f = lb + (1.0 - lb) * _sigmoid(proj(C_HF))
    hq = proj(C_HQ)
    o_h = _chunk_recurrence(hq * _sigmoid(hq), 1.0 - f, proj(C_HI), jnp.log(f), sh_ref, tril, causal,
                            bmh_ref[...], HGRN_HEADS, HGRN_HEAD_F, HGRN_HEAD_I)
    y_h = _gated_head_rmsnorm(o_h, proj(C_HOG), hgg_ref[...], HGRN_HEADS, HGRN_HEAD_I)

    u = proj(C_CC) * proj(C_CH)
    prev = cc_ref[...]
    row = lax.broadcasted_iota(jnp.int32, (rows, 1), 0)
    u1 = jnp.where(row == 0, prev[SUBLANES - 1:SUBLANES], pltpu.roll(u, 1, 0))
    u2 = jnp.where(row == 0, prev[SUBLANES - 2:SUBLANES - 1],
                   jnp.where(row == 1, prev[SUBLANES - 1:SUBLANES], pltpu.roll(u, 2, 0)))
    cc_ref[...] = u[rows - SUBLANES:rows]
    cw = convw_ref[...]
    y_conv = proj(C_CB) * (cw[0:1] * u2 + cw[1:2] * u1 + cw[2:3] * u)

    o1, o2 = GLA_V_DIM, GLA_V_DIM + HGRN_I_DIM
    mix = (_dot(y_gla.astype(BF16), wout_ref[0:o1, :]) + _dot(y_h.astype(BF16), wout_ref[o1:o2, :])
           + _dot(y_conv.astype(BF16), wout_ref[o2:o2 + CONV_CHANNELS, :]))
    o_ref[...] = _layer_norm(alpha * h + mix, mg_ref[...], mb_ref[...])


def _mixer(x, ln_g, ln_b, w_in_r, w_exp, b_gate, gla_g, lb_raw, hgrn_g, conv_w, w_out, mg, mb,
           consts, *, layer, pre_ln, alpha):
    bsz, t, d = x.shape
    rows = min(MIX_ROWS, t)
    assert t % rows == 0 and rows % CHUNK == 0
    tril, causal, bmg, bmh = consts
    depth = w_in_r.shape[0]

    def lay(shape):
        return pl.BlockSpec((None,) + shape, lambda b, i: (layer,) + (0,) * len(shape))

    def whole(a):
        return pl.BlockSpec(a.shape, lambda b, i: (0,) * a.ndim)

    return pl.pallas_call(
        functools.partial(_mixer_kernel, layer=layer, pre_ln=pre_ln, alpha=alpha),
        grid=(bsz, t // rows),
        in_specs=[
            pl.BlockSpec((None, rows, d), lambda b, i: (b, i, 0)),
            whole(ln_g), whole(ln_b),
            lay((d, D_IN_PAD)), lay((LANES, GLA_K_DIM)), lay((1, GLA_K_DIM)), lay((1, GLA_V_DIM)),
            whole(lb_raw), lay((1, HGRN_I_DIM)), lay((CONV_WIDTH, CONV_CHANNELS)), lay((d, d)),
            lay((1, d)), lay((1, d)),
            whole(tril), whole(causal), whole(bmg), whole(bmh),
        ],
        out_specs=pl.BlockSpec((None, rows, d), lambda b, i: (b, i, 0)),
        out_shape=jax.ShapeDtypeStruct((bsz, t, d), F32),
        scratch_shapes=[pltpu.VMEM((GLA_V_DIM, GLA_K_DIM), F32), pltpu.VMEM((HGRN_I_DIM, HGRN_F_DIM), F32),
                        pltpu.VMEM((SUBLANES, CONV_CHANNELS), F32)],
        compiler_params=pltpu.CompilerParams(dimension_semantics=("arbitrary", "arbitrary"),
                                             vmem_limit_bytes=VMEM_LIMIT_BYTES),
        name=f"mixer_l{layer}",
    )(x, ln_g, ln_b, w_in_r, w_exp, b_gate, gla_g, lb_raw, hgrn_g, conv_w, w_out, mg, mb, tril, causal, bmg, bmh)


def _mixer_consts(rows):
    r = jnp.arange(rows)
    tril = ((r[:, None] // CHUNK == r[None, :] // CHUNK) & (r[None, :] <= r[:, None])).astype(BF16)
    i = jnp.arange(GLA_HEADS * CHUNK) % CHUNK
    causal = (jnp.arange(CHUNK)[None, :] <= i[:, None]).astype(F32)
    bmg = (jnp.arange(GLA_V_DIM)[:, None] // GLA_HEAD_V == jnp.arange(GLA_K_DIM)[None, :] // GLA_HEAD_K).astype(F32)
    bmh = (jnp.arange(HGRN_I_DIM)[:, None] // HGRN_HEAD_I == jnp.arange(HGRN_F_DIM)[None, :] // HGRN_HEAD_F).astype(F32)
    return tril, causal, bmg, bmh


def _router_kernel(h_ref, wr_ref, br_ref, ls_ref, idx_ref, gate_ref, rank_ref, cnt_ref, carry_ref):
    @pl.when(pl.program_id(0) == 0)
    def _():
        carry_ref[...] = jnp.zeros_like(carry_ref)

    logits = lax.dot_general(h_ref[...], wr_ref[...], (((1,), (0,)), ((), ())),
                             precision=lax.Precision.HIGHEST, preferred_element_type=F32) + br_ref[...]
    lane = lax.broadcasted_iota(jnp.int32, logits.shape, 1).astype(F32)
    neg = jnp.float32(-jnp.inf)
    l = jnp.where(lane < N_EXPERTS, logits, neg)
    tops, idxs = [], []
    for _ in range(TOP_K):
        m = jnp.max(l, axis=-1, keepdims=True)
        ik = jnp.min(jnp.where(l == m, lane, float(LANES)), axis=-1, keepdims=True)
        tops.append(m)
        idxs.append(ik)
        l = jnp.where(lane == ik, neg, l)
    es = [jnp.exp(t - tops[0]) for t in tops]
    denom = es[0] + es[1] + es[2] + es[3]
    onehot = jnp.zeros_like(logits)
    for ik in idxs:
        onehot = onehot + jnp.where(lane == ik, 1.0, 0.0)
    carry = carry_ref[...]
    before = _dot(ls_ref[...], onehot.astype(BF16)) + carry
    ranks = [jnp.sum(jnp.where(lane == ik, before, 0.0), axis=-1, keepdims=True) for ik in idxs]
    carry = carry + jnp.sum(onehot, axis=0, keepdims=True)
    carry_ref[...] = carry
    cnt_ref[...] = carry.astype(jnp.int32)
    idx_ref[...] = _columns_to_lanes(idxs, TOP_K).astype(jnp.int32)
    gate_ref[...] = _columns_to_lanes([e / denom for e in es], TOP_K)
    rank_ref[...] = _columns_to_lanes(ranks, TOP_K).astype(jnp.int32)


def _router(h, w_router_p, b_router_p, layer):
    n, d = h.shape
    rows = min(ROUTE_ROWS, n)
    assert n % rows == 0
    r = jnp.arange(rows)
    lstrict = (r[None, :] < r[:, None]).astype(BF16)
    pair = pl.BlockSpec((rows, TOP_K), lambda i: (i, 0))
    return pl.pallas_call(
        _router_kernel,
        grid=(n // rows,),
        in_specs=[pl.BlockSpec((rows, d), lambda i: (i, 0)),
                  pl.BlockSpec((None, d, LANES), lambda i: (layer, 0, 0)),
                  pl.BlockSpec((None, 1, LANES), lambda i: (layer, 0, 0)),
                  pl.BlockSpec((rows, rows), lambda i: (0, 0))],
        out_specs=[pair, pair, pair, pl.BlockSpec((1, LANES), lambda i: (0, 0))],
        out_shape=[jax.ShapeDtypeStruct((n, TOP_K), jnp.int32), jax.ShapeDtypeStruct((n, TOP_K), F32),
                   jax.ShapeDtypeStruct((n, TOP_K), jnp.int32), jax.ShapeDtypeStruct((1, LANES), jnp.int32)],
        scratch_shapes=[pltpu.VMEM((1, LANES), F32)],
        compiler_params=pltpu.CompilerParams(dimension_semantics=("arbitrary",), vmem_limit_bytes=VMEM_LIMIT_BYTES),
        name=f"router_l{layer}",
    )(h, w_router_p, b_router_p, lstrict)


def _slots_kernel(idx_ref, rank_ref, start_ref, dest_ref):
    idx = idx_ref[...]
    lane = lax.broadcasted_iota(jnp.int32, (idx.shape[0], LANES), 1)
    start = start_ref[...].astype(F32)
    cols = [jnp.sum(jnp.where(lane == idx[:, k:k + 1], start, 0.0), axis=-1, keepdims=True) for k in range(TOP_K)]
    dest_ref[...] = _columns_to_lanes(cols, TOP_K).astype(jnp.int32) + rank_ref[...]


def _slots(idx, rank, start, layer):
    n = idx.shape[0]
    rows = min(ROUTE_ROWS, n)
    pair = pl.BlockSpec((rows, TOP_K), lambda i: (i, 0))
    return pl.pallas_call(
        _slots_kernel,
        grid=(n // rows,),
        in_specs=[pair, pair, pl.BlockSpec((1, LANES), lambda i: (0, 0))],
        out_specs=pair,
        out_shape=jax.ShapeDtypeStruct((n, TOP_K), jnp.int32),
        compiler_params=pltpu.CompilerParams(dimension_semantics=("arbitrary",)),
        name=f"slots_l{layer}",
    )(idx, rank, start)


def _dispatch_kernel(pend_ref, pcnt_ref, dest_ref, h_ref, xs_ref, zbuf_ref, sem, zsem, *, block_rows):
    rows = h_ref.shape[0]

    def fill(first_row):
        start = pl.multiple_of(first_row, block_rows)
        return pltpu.make_async_copy(zbuf_ref, xs_ref.at[pl.ds(start, block_rows)], zsem)

    @pl.when(pl.program_id(0) == 0)
    def _():
        zbuf_ref[...] = jnp.zeros_like(zbuf_ref)
        used_blocks = pend_ref[N_EXPERTS - 1] // block_rows
        all_blocks = xs_ref.shape[0] // block_rows
        for e in range(N_EXPERTS):
            @pl.when(pcnt_ref[e] > 0)
            def _():
                fill(pend_ref[e] - block_rows).start()
        lax.fori_loop(used_blocks, all_blocks, lambda j, c: (fill(j * block_rows).start(), c)[1], 0)
        for e in range(N_EXPERTS):
            @pl.when(pcnt_ref[e] > 0)
            def _():
                fill(pend_ref[e] - block_rows).wait()
        lax.fori_loop(used_blocks, all_blocks, lambda j, c: (fill(j * block_rows).wait(), c)[1], 0)

    def row_copy(t, k):
        return pltpu.make_async_copy(h_ref.at[pl.ds(t, 1)], xs_ref.at[pl.ds(dest_ref[t * TOP_K + k], 1)], sem)

    def body(t, carry):
        for k in range(TOP_K):
            row_copy(t, k).start()
        return carry

    lax.fori_loop(0, rows, body, 0, unroll=8)
    for k in range(TOP_K):
        pltpu.make_async_copy(h_ref, xs_ref.at[pl.ds(0, rows)], sem).wait()


def _dispatch(h, dest_flat, pend, pcnt, total_rows, layer):
    n, d = h.shape
    rows = min(DISPATCH_ROWS, n)
    assert n % rows == 0
    return pl.pallas_call(
        functools.partial(_dispatch_kernel, block_rows=EXPERT_ROWS),
        grid_spec=pltpu.PrefetchScalarGridSpec(
            num_scalar_prefetch=2,
            grid=(n // rows,),
            in_specs=[pl.BlockSpec((rows * TOP_K,), lambda i, *_: (i,), memory_space=pltpu.SMEM),
                      pl.BlockSpec((rows, d), lambda i, *_: (i, 0))],
            out_specs=pl.BlockSpec(memory_space=pl.ANY),
            scratch_shapes=[pltpu.VMEM((EXPERT_ROWS, d), F32), pltpu.SemaphoreType.DMA(()),
                            pltpu.SemaphoreType.DMA(())],
        ),
        out_shape=jax.ShapeDtypeStruct((total_rows, d), F32),
        compiler_params=pltpu.CompilerParams(dimension_semantics=("arbitrary",), vmem_limit_bytes=VMEM_LIMIT_BYTES),
        name=f"dispatch_l{layer}",
    )(pend, pcnt, dest_flat, h)


def _combine_kernel(dest_ref, h_ref, gate_ref, lg_ref, lb_ref, y_ref, o_ref, ybuf_ref, sem, *, alpha):
    rows = h_ref.shape[0]

    def row_copy(t, k):
        return pltpu.make_async_copy(y_ref.at[pl.ds(dest_ref[t * TOP_K + k], 1)], ybuf_ref.at[k, pl.ds(t, 1)], sem)

    def body(t, carry):
        for k in range(TOP_K):
            row_copy(t, k).start()
        return carry

    lax.fori_loop(0, rows, body, 0, unroll=8)
    for k in range(TOP_K):
        pltpu.make_async_copy(y_ref.at[pl.ds(0, rows)], ybuf_ref.at[k], sem).wait()
    gates = gate_ref[...]
    ffn = gates[:, 0:1] * ybuf_ref[0]
    for k in range(1, TOP_K):
        ffn = ffn + gates[:, k:k + 1] * ybuf_ref[k]
    o_ref[...] = _layer_norm(alpha * h_ref[...] + ffn, lg_ref[...], lb_ref[...])


def _combine(h, gates, dest_flat, ys, ln_g, ln_b, layer, alpha):
    n, d = h.shape
    rows = min(COMBINE_ROWS, n)
    assert n % rows == 0
    return pl.pallas_call(
        functools.partial(_combine_kernel, alpha=alpha),
        grid=(n // rows,),
        in_specs=[pl.BlockSpec((rows * TOP_K,), lambda i: (i,), memory_space=pltpu.SMEM),
                  pl.BlockSpec((rows, d), lambda i: (i, 0)),
                  pl.BlockSpec((rows, TOP_K), lambda i: (i, 0)),
                  pl.BlockSpec((None, 1, d), lambda i: (layer, 0, 0)),
                  pl.BlockSpec((None, 1, d), lambda i: (layer, 0, 0)),
                  pl.BlockSpec(memory_space=pl.ANY)],
        out_specs=pl.BlockSpec((rows, d), lambda i: (i, 0)),
        out_shape=jax.ShapeDtypeStruct((n, d), F32),
        scratch_shapes=[pltpu.VMEM((TOP_K, rows, d), F32), pltpu.SemaphoreType.DMA(())],
        compiler_params=pltpu.CompilerParams(dimension_semantics=("arbitrary",), vmem_limit_bytes=VMEM_LIMIT_BYTES),
        name=f"combine_l{layer}",
    )(dest_flat, h, gates, ln_g, ln_b, ys)


def _expert_kernel(be_ref, nact_ref, x_ref, wgu_ref, bgu_ref, wd_ref, bd_ref, y_ref):
    @pl.when(pl.program_id(0) < nact_ref[0])
    def _():
        d_ff = wd_ref.shape[0]
        gu = _dot(x_ref[...].astype(BF16), wgu_ref[...]) + bgu_ref[...]
        gate = jnp.minimum(gu[:, :d_ff], SWIGLU_LIMIT)
        up = jnp.clip(gu[:, d_ff:], -SWIGLU_LIMIT, SWIGLU_LIMIT)
        act = (up + 1.0) * gate * _sigmoid(SWIGLU_ALPHA * gate)
        y_ref[...] = _dot(act.astype(BF16), wd_ref[...]) + bd_ref[...]

    @pl.when(pl.program_id(0) >= nact_ref[0])
    def _():
        y_ref[...] = jnp.zeros_like(y_ref)


def _experts(xs, blk_expert, nact, wgu, bgu, wd, bd, layer):
    total_rows, d = xs.shape
    nblk = total_rows // EXPERT_ROWS
    d_ff = wd.shape[2]

    def row_block(j, be, na):
        return (j, 0)

    def of_expert(j, be, na):
        return (layer, be[j], 0, 0)

    return pl.pallas_call(
        _expert_kernel,
        grid_spec=pltpu.PrefetchScalarGridSpec(
            num_scalar_prefetch=2,
            grid=(nblk,),
            in_specs=[pl.BlockSpec((EXPERT_ROWS, d), row_block),
                      pl.BlockSpec((None, None, d, 2 * d_ff), of_expert),
                      pl.BlockSpec((None, None, 1, 2 * d_ff), of_expert),
                      pl.BlockSpec((None, None, d_ff, d), of_expert),
                      pl.BlockSpec((None, None, 1, d), of_expert)],
            out_specs=pl.BlockSpec((EXPERT_ROWS, d), row_block),
        ),
        out_shape=jax.ShapeDtypeStruct((total_rows, d), F32),
        compiler_params=pltpu.CompilerParams(dimension_semantics=("arbitrary",), vmem_limit_bytes=VMEM_LIMIT_BYTES),
        name=f"experts_l{layer}",
    )(blk_expert, nact, xs, wgu, bgu, wd, bd)


def _moe(h, w_router_p, b_router_p, wgu, bgu, wd, bd, ln_g, ln_b, layer, alpha):
    n, d = h.shape
    idx, gates, rank, counts = _router(h, w_router_p, b_router_p, layer)
    counts = counts[0, :N_EXPERTS]
    padded = (counts + EXPERT_ROWS - 1) // EXPERT_ROWS * EXPERT_ROWS
    pend = jnp.cumsum(padded).astype(jnp.int32)
    pstart = pend - padded
    nblk = n * TOP_K // EXPERT_ROWS + N_EXPERTS
    nact = (pend[-1] // EXPERT_ROWS).reshape(1).astype(jnp.int32)
    blk = jnp.minimum(jnp.arange(nblk, dtype=jnp.int32), nact[0] - 1) * EXPERT_ROWS
    blk_expert = jnp.minimum(jnp.sum(pend[None, :] <= blk[:, None], axis=1), N_EXPERTS - 1).astype(jnp.int32)
    start_row = jnp.zeros((1, LANES), jnp.int32).at[0, :N_EXPERTS].set(pstart)

    dest = _slots(idx, rank, start_row, layer).reshape(n * TOP_K)
    xs = _dispatch(h, dest, pend, padded.astype(jnp.int32), nblk * EXPERT_ROWS, layer)
    ys = _experts(xs, blk_expert, nact, wgu, bgu, wd, bd, layer)
    return _combine(h, gates, dest, ys, ln_g, ln_b, layer, alpha)


def _relayout_w_in(w_in):
    depth, d, _ = w_in.shape
    out = jnp.zeros((depth, d, D_IN_PAD), BF16)
    for g in range(len(IN_SPLITS)):
        lo, hi = _COLS[g]
        out = out.at[:, :, lo:hi].set(w_in[:, :, _SRC_OFF[g]:_SRC_OFF[g] + IN_SPLITS[g]].astype(BF16))
    return out


def kernel(x, ln_in_g, ln_in_b, w_in, w_gla_gate_expand, b_gla_gate, gla_norm_g, hgrn_lb_raw, hgrn_norm_g, conv_w, w_out, ln_mix_g, ln_mix_b, w_router, b_router, w_gate_up, b_gate_up, w_down, b_down, ln_ffn_g, ln_ffn_b):
    bsz, t, d = x.shape
    depth = w_in.shape[0]
    alpha = (2 * depth) ** 0.25
    n = bsz * t

    w_in_r = _relayout_w_in(w_in)
    w_exp = jnp.zeros((depth, LANES, GLA_K_DIM), F32).at[:, :GLA_GATE_RANK, :].set(w_gla_gate_expand)
    b_gate = b_gla_gate.reshape(depth, 1, GLA_K_DIM)
    gla_g = jnp.tile(gla_norm_g, (1, GLA_HEADS)).reshape(depth, 1, GLA_V_DIM)
    hgrn_g = jnp.tile(hgrn_norm_g, (1, HGRN_HEADS)).reshape(depth, 1, HGRN_I_DIM)
    w_out_b = w_out.astype(BF16)
    w_router_p = jnp.zeros((depth, d, LANES), F32).at[:, :, :N_EXPERTS].set(w_router)
    b_router_p = jnp.zeros((depth, 1, LANES), F32).at[:, 0, :N_EXPERTS].set(b_router)
    wgu = w_gate_up.astype(BF16)
    wd = w_down.astype(BF16)
    bgu = b_gate_up.reshape(depth, N_EXPERTS, 1, -1)
    bd = b_down.reshape(depth, N_EXPERTS, 1, -1)
    row = lambda a: a.reshape(depth, 1, d)
    consts = _mixer_consts(min(MIX_ROWS, t))

    h = x
    for l in range(depth):
        h = _mixer(h, ln_in_g.reshape(1, d), ln_in_b.reshape(1, d), w_in_r, w_exp, b_gate, gla_g, hgrn_lb_raw,
                   hgrn_g, conv_w, w_out_b, row(ln_mix_g), row(ln_mix_b), consts,
                   layer=l, pre_ln=(l == 0), alpha=alpha)
        h = _moe(h.reshape(n, d), w_router_p, b_router_p, wgu, bgu, wd, bd, row(ln_ffn_g), row(ln_ffn_b),
                 l, alpha).reshape(bsz, t, d)
    return h
```

```python
import functools

import jax
import jax.numpy as jnp
from jax import lax
from jax.experimental import pallas as pl
from jax.experimental.pallas import tpu as pltpu

F32 = jnp.float32
BF16 = jnp.bfloat16
U32 = jnp.uint32
I32 = jnp.int32

GLA_HEADS = 4
GLA_HEAD_K = 64
GLA_HEAD_V = 128
GLA_GATE_RANK = 16
GLA_GATE_NORMALIZER = 16.0
HGRN_HEADS = 4
HGRN_HEAD_F = 64
HGRN_HEAD_I = 64
CONV_CHANNELS = 256
CONV_WIDTH = 3
CHUNK = 64
N_EXPERTS = 32
TOP_K = 4
SWIGLU_LIMIT = 7.0
SWIGLU_ALPHA = 1.702
LN_EPS = 1e-5
RMS_EPS = 1e-6

GLA_K_DIM = GLA_HEADS * GLA_HEAD_K
GLA_V_DIM = GLA_HEADS * GLA_HEAD_V
HGRN_F_DIM = HGRN_HEADS * HGRN_HEAD_F
HGRN_I_DIM = HGRN_HEADS * HGRN_HEAD_I
IN_SPLITS = (GLA_K_DIM, GLA_K_DIM, GLA_V_DIM, GLA_GATE_RANK, GLA_V_DIM,
             HGRN_F_DIM, HGRN_F_DIM, HGRN_I_DIM, HGRN_I_DIM,
             CONV_CHANNELS, CONV_CHANNELS, CONV_CHANNELS)

LANES = 128
SUBLANES = 8
VMEM_LIMIT_BYTES = 56 * 1024 * 1024

MIX_ROWS = 256
PERM_ROWS = 256
PERM_GRAN = 8
PERM_LOCAL = PERM_ROWS * TOP_K + N_EXPERTS * 2 * PERM_GRAN
EXPERT_ROWS = 512
SMEM_TABLE = 1024
TABLE_GROUP = SMEM_TABLE // LANES

_ORDER = (0, 1, 2, 4, 5, 6, 7, 8, 9, 10, 11, 3)
_SRC_OFF = [sum(IN_SPLITS[:i]) for i in range(len(IN_SPLITS))]
_COLS = {}
_off = 0
for _g in _ORDER:
    _w = IN_SPLITS[_g]
    _COLS[_g] = (_off, _off + _w)
    _off += -(-_w // LANES) * LANES
D_IN_PAD = _off
(C_GQ, C_GK, C_GV, C_GLR, C_GOG, C_HQ, C_HF, C_HI, C_HOG, C_CB, C_CC, C_CH) = (_COLS[i] for i in range(12))


def _layer_norm(x, g, b):
    mu = jnp.mean(x, axis=-1, keepdims=True)
    xc = x - mu
    var = jnp.mean(xc * xc, axis=-1, keepdims=True)
    return xc * lax.rsqrt(var + LN_EPS) * g + b


def _sigmoid(x):
    return 1.0 / (1.0 + jnp.exp(-x))


def _dot(a, b):
    return jnp.dot(a, b, preferred_element_type=F32)


def _dot_nt(a, b):
    return lax.dot_general(a, b, (((1,), (1,)), ((), ())), preferred_element_type=F32)


def _dot_tn(a, b):
    return lax.dot_general(a, b, (((0,), (0,)), ((), ())), preferred_element_type=F32)


def _dot_f32(a, b):
    return lax.dot_general(a, b, (((1,), (0,)), ((), ())), precision=lax.Precision.HIGHEST,
                           preferred_element_type=F32)


def _split_dot(a01, x):
    hi = x.astype(BF16)
    lo = (x - hi.astype(F32)).astype(BF16)
    return _dot(a01, hi) + _dot(a01, lo)


def _columns_to_lanes(cols, width):
    lane = lax.broadcasted_iota(I32, (cols[0].shape[0], width), 1)
    out = jnp.broadcast_to(cols[-1], lane.shape)
    for k in range(width - 2, -1, -1):
        out = jnp.where(lane == k, cols[k], out)
    return out


def _pack_bf16_pair(lo, hi):
    return lax.bitcast_convert_type(hi, U32) | (lax.bitcast_convert_type(lo, U32) >> 16)


def _unpack_bf16_pair(word):
    lo = lax.bitcast_convert_type(word << 16, F32).astype(BF16)
    hi = lax.bitcast_convert_type(word & jnp.uint32(0xFFFF0000), F32).astype(BF16)
    return lo, hi


def _round_bf16(x):
    return x.astype(BF16).astype(F32)


def _chunk_recurrence(q, k, v, g, st_ref, tril, causal, bmask, heads, dk, dv):
    rows = q.shape[0]
    gc = _split_dot(tril, g)
    lane_k = lax.broadcasted_iota(I32, (1, heads * dk), 1) // dk
    lane_v = lax.broadcasted_iota(I32, (1, heads * dv), 1) // dv
    st = st_ref[...]
    outs = []
    for c in range(rows // CHUNK):
        sl = slice(c * CHUNK, (c + 1) * CHUNK)
        gcc, qc, kc, vc = gc[sl], q[sl], k[sl], v[sl]
        mid = gcc[CHUNK // 2 - 1:CHUNK // 2]
        last = gcc[CHUNK - 1:CHUNK]
        qt = qc * jnp.exp(gcc - mid)
        kt = kc * jnp.exp(mid - gcc)
        qi = qc * jnp.exp(gcc)
        ks = kc * jnp.exp(last - gcc)
        vb = vc.astype(BF16)
        qstack = jnp.concatenate([jnp.where(lane_k == h, qt, 0.0) for h in range(heads)], axis=0)
        a = _dot_nt(qstack.astype(BF16), kt.astype(BF16))
        a = jnp.where(causal, a, 0.0)
        o_all = _dot(a.astype(BF16), vb)
        o_intra = jnp.where(lane_v == 0, o_all[0:CHUNK], 0.0)
        for h in range(1, heads):
            o_intra = o_intra + jnp.where(lane_v == h, o_all[h * CHUNK:(h + 1) * CHUNK], 0.0)
        o_inter = _dot_nt(qi.astype(BF16), st.astype(BF16))
        u = _dot_tn(vb, ks.astype(BF16))
        st = jnp.exp(last) * st + u * bmask
        outs.append(o_intra + o_inter)
    st_ref[...] = st
    return jnp.concatenate(outs, axis=0)


def _gated_head_rmsnorm(o, gate, g_tiled, heads, dv):
    lane_v = lax.broadcasted_iota(I32, (1, heads * dv), 1) // dv
    sq = o * o
    ms = jnp.zeros_like(o)
    for h in range(heads):
        m = lane_v == h
        ms = jnp.where(m, jnp.sum(jnp.where(m, sq, 0.0), axis=-1, keepdims=True), ms)
    ms = ms * (1.0 / dv)
    return o * lax.rsqrt(ms + RMS_EPS) * g_tiled * (gate * _sigmoid(gate))


def _mixer_kernel(x_ref, lng_ref, lnb_ref, win_ref, wexp_ref, bgate_ref, glag_ref, lbraw_ref, hgg_ref,
                  convw_ref, wout_ref, mg_ref, mb_ref, tril_ref, causal_ref, bmg_ref, bmh_ref,
                  o_ref, sg_ref, sh_ref, cc_ref, *, layer, pre_ln, alpha):
    @pl.when(pl.program_id(1) == 0)
    def _():
        sg_ref[...] = jnp.zeros_like(sg_ref)
        sh_ref[...] = jnp.zeros_like(sh_ref)
        cc_ref[...] = jnp.zeros_like(cc_ref)

    x = x_ref[...]
    h = _layer_norm(x, lng_ref[...], lnb_ref[...]) if pre_ln else x
    hb = h.astype(BF16)
    rows = x.shape[0]

    def proj(cols):
        return _dot(hb, win_ref[:, cols[0]:cols[0] + -(-(cols[1] - cols[0]) // LANES) * LANES])

    tril = tril_ref[...]
    causal = causal_ref[...] > 0.5

    q = proj(C_GQ) * (GLA_HEAD_K ** -0.5)
    gate_logits = _dot_f32(proj(C_GLR), wexp_ref[...]) + bgate_ref[...]
    log_a = (jnp.minimum(gate_logits, 0.0) - jnp.log(1.0 + jnp.exp(-jnp.abs(gate_logits)))) * (1.0 / GLA_GATE_NORMALIZER)
    o_gla = _chunk_recurrence(q, proj(C_GK), proj(C_GV), log_a, sg_ref, tril, causal, bmg_ref[...],
                              GLA_HEADS, GLA_HEAD_K, GLA_HEAD_V)
    y_gla = _gated_head_rmsnorm(o_gla, proj(C_GOG), glag_ref[...], GLA_HEADS, GLA_HEAD_V)

    raw = lbraw_ref[...]
    e = jnp.exp(raw - jnp.max(raw, axis=0, keepdims=True))
    lb = jnp.zeros_like(e[0:1])
    for i in range(1, layer + 1):
        lb = lb + e[i:i + 1]
    lb = lb / jnp.sum(e, axis=0, keepdims=True)
    f = lb + (1.0 - lb) * _sigmoid(proj(C_HF))
    hq = proj(C_HQ)
    o_h = _chunk_recurrence(hq * _sigmoid(hq), 1.0 - f, proj(C_HI), jnp.log(f), sh_ref, tril, causal,
                            bmh_ref[...], HGRN_HEADS, HGRN_HEAD_F, HGRN_HEAD_I)
    y_h = _gated_head_rmsnorm(o_h, proj(C_HOG), hgg_ref[...], HGRN_HEADS, HGRN_HEAD_I)

    u = proj(C_CC) * proj(C_CH)
    prev = cc_ref[...]
    row = lax.broadcasted_iota(I32, (rows, 1), 0)
    u1 = jnp.where(row == 0, prev[SUBLANES - 1:SUBLANES], pltpu.roll(u, 1, 0))
    u2 = jnp.where(row == 0, prev[SUBLANES - 2:SUBLANES - 1],
                   jnp.where(row == 1, prev[SUBLANES - 1:SUBLANES], pltpu.roll(u, 2, 0)))
    cc_ref[...] = u[rows - SUBLANES:rows]
    cw = convw_ref[...]
    y_conv = proj(C_CB) * (cw[0:1] * u2 + cw[1:2] * u1 + cw[2:3] * u)

    o1, o2 = GLA_V_DIM, GLA_V_DIM + HGRN_I_DIM
    mix = (_dot(y_gla.astype(BF16), wout_ref[0:o1, :]) + _dot(y_h.astype(BF16), wout_ref[o1:o2, :])
           + _dot(y_conv.astype(BF16), wout_ref[o2:o2 + CONV_CHANNELS, :]))
    o_ref[...] = _layer_norm(alpha * h + mix, mg_ref[...], mb_ref[...])


def _mixer(x, ln_g, ln_b, w_in_r, w_exp, b_gate, gla_g, lb_raw, hgrn_g, conv_w, w_out, mg, mb,
           consts, *, layer, pre_ln, alpha):
    bsz, t, d = x.shape
    rows = min(MIX_ROWS, t)
    assert t % rows == 0 and rows % CHUNK == 0
    tril, causal, bmg, bmh = consts

    def lay(shape):
        return pl.BlockSpec((None,) + shape, lambda b, i: (layer,) + (0,) * len(shape))

    def whole(a):
        return pl.BlockSpec(a.shape, lambda b, i: (0,) * a.ndim)

    return pl.pallas_call(
        functools.partial(_mixer_kernel, layer=layer, pre_ln=pre_ln, alpha=alpha),
        grid=(bsz, t // rows),
        in_specs=[
            pl.BlockSpec((None, rows, d), lambda b, i: (b, i, 0)),
            whole(ln_g), whole(ln_b),
            lay((d, D_IN_PAD)), lay((LANES, GLA_K_DIM)), lay((1, GLA_K_DIM)), lay((1, GLA_V_DIM)),
            whole(lb_raw), lay((1, HGRN_I_DIM)), lay((CONV_WIDTH, CONV_CHANNELS)), lay((d, d)),
            lay((1, d)), lay((1, d)),
            whole(tril), whole(causal), whole(bmg), whole(bmh),
        ],
        out_specs=pl.BlockSpec((None, rows, d), lambda b, i: (b, i, 0)),
        out_shape=jax.ShapeDtypeStruct((bsz, t, d), F32),
        scratch_shapes=[pltpu.VMEM((GLA_V_DIM, GLA_K_DIM), F32), pltpu.VMEM((HGRN_I_DIM, HGRN_F_DIM), F32),
                        pltpu.VMEM((SUBLANES, CONV_CHANNELS), F32)],
        compiler_params=pltpu.CompilerParams(dimension_semantics=("arbitrary", "arbitrary"),
                                             vmem_limit_bytes=VMEM_LIMIT_BYTES),
        name=f"mixer_l{layer}",
    )(x, ln_g, ln_b, w_in_r, w_exp, b_gate, gla_g, lb_raw, hgrn_g, conv_w, w_out, mg, mb, tril, causal, bmg, bmh)


def _mixer_consts(rows):
    r = jnp.arange(rows)
    tril = ((r[:, None] // CHUNK == r[None, :] // CHUNK) & (r[None, :] <= r[:, None])).astype(BF16)
    i = jnp.arange(GLA_HEADS * CHUNK) % CHUNK
    causal = (jnp.arange(CHUNK)[None, :] <= i[:, None]).astype(F32)
    bmg = (jnp.arange(GLA_V_DIM)[:, None] // GLA_HEAD_V == jnp.arange(GLA_K_DIM)[None, :] // GLA_HEAD_K).astype(F32)
    bmh = (jnp.arange(HGRN_I_DIM)[:, None] // HGRN_HEAD_I == jnp.arange(HGRN_F_DIM)[None, :] // HGRN_HEAD_F).astype(F32)
    return tril, causal, bmg, bmh


def _router_kernel(h_ref, wr_ref, br_ref, ls_ref, ut_ref, gate_ref, pos_ref, cnt_ref, off_ref, before_ref,
                   total_ref, carry_ref):
    @pl.when(pl.program_id(0) == 0)
    def _():
        carry_ref[...] = jnp.zeros_like(carry_ref)

    logits = _dot_f32(h_ref[...], wr_ref[...]) + br_ref[...]
    lane = lax.broadcasted_iota(I32, logits.shape, 1).astype(F32)
    neg = jnp.float32(-jnp.inf)
    l = jnp.where(lane < N_EXPERTS, logits, neg)
    tops, idxs = [], []
    for _ in range(TOP_K):
        m = jnp.max(l, axis=-1, keepdims=True)
        ik = jnp.min(jnp.where(l == m, lane, float(LANES)), axis=-1, keepdims=True)
        tops.append(m)
        idxs.append(ik)
        l = jnp.where(lane == ik, neg, l)
    es = [jnp.exp(t - tops[0]) for t in tops]
    denom = es[0] + es[1] + es[2] + es[3]
    onehot = jnp.zeros_like(logits)
    for ik in idxs:
        onehot = onehot + jnp.where(lane == ik, 1.0, 0.0)

    carry = carry_ref[...]
    earlier = _dot(ls_ref[...], onehot.astype(BF16))
    cnt = jnp.sum(onehot, axis=0, keepdims=True)
    phase = carry - PERM_GRAN * jnp.floor(carry * (1.0 / PERM_GRAN))
    run = jnp.where(cnt > 0.0, jnp.ceil((phase + cnt) * (1.0 / PERM_GRAN)) * PERM_GRAN, 0.0)
    off = _dot_f32(jnp.broadcast_to(run, (SUBLANES, LANES)), ut_ref[...])[0:1]
    local = earlier + (off + phase)
    pos = [jnp.sum(jnp.where(lane == ik, local, 0.0), axis=-1, keepdims=True) for ik in idxs]

    cnt_ref[...] = cnt.astype(I32)
    off_ref[...] = off.astype(I32)
    before_ref[...] = carry.astype(I32)
    carry = carry + cnt
    carry_ref[...] = carry
    total_ref[...] = carry.astype(I32)
    gate_ref[...] = _columns_to_lanes([e / denom for e in es], TOP_K)
    pos_ref[...] = _columns_to_lanes(pos, TOP_K).astype(I32)


def _router(h, w_router_p, b_router_p, layer):
    n, d = h.shape
    rows = min(PERM_ROWS, n)
    assert n % rows == 0
    nblk = n // rows
    r = jnp.arange(rows)
    lstrict = (r[None, :] < r[:, None]).astype(BF16)
    e = jnp.arange(LANES)
    upper = (e[:, None] < e[None, :]).astype(F32)
    pair = pl.BlockSpec((rows, TOP_K), lambda i: (i, 0))
    table = pl.BlockSpec((None, 1, LANES), lambda i: (i, 0, 0))
    table_shape = jax.ShapeDtypeStruct((nblk, 1, LANES), I32)
    return pl.pallas_call(
        _router_kernel,
        grid=(nblk,),
        in_specs=[pl.BlockSpec((rows, d), lambda i: (i, 0)),
                  pl.BlockSpec((None, d, LANES), lambda i: (layer, 0, 0)),
                  pl.BlockSpec((None, 1, LANES), lambda i: (layer, 0, 0)),
                  pl.BlockSpec((rows, rows), lambda i: (0, 0)),
                  pl.BlockSpec((LANES, LANES), lambda i: (0, 0))],
        out_specs=[pair, pair, table, table, table, pl.BlockSpec((1, LANES), lambda i: (0, 0))],
        out_shape=[jax.ShapeDtypeStruct((n, TOP_K), F32), jax.ShapeDtypeStruct((n, TOP_K), I32),
                   table_shape, table_shape, table_shape, jax.ShapeDtypeStruct((1, LANES), I32)],
        scratch_shapes=[pltpu.VMEM((1, LANES), F32)],
        compiler_params=pltpu.CompilerParams(dimension_semantics=("arbitrary",), vmem_limit_bytes=VMEM_LIMIT_BYTES),
        name=f"router_l{layer}",
    )(h, w_router_p, b_router_p, lstrict, upper)


def _for_each_run(cnt_ref, off_ref, dst_ref, table_row, chunk_fn, row_fn=None):
    base = table_row * LANES
    chunks = jnp.int32(0)
    rows = jnp.int32(0)
    for e in range(N_EXPERTS):
        n = cnt_ref[base + e]
        loc = off_ref[base + e]
        dst = dst_ref[base + e]
        phase = dst & (PERM_GRAN - 1)
        if row_fn is None:
            n_head = jnp.int32(0)
            loc0, dst0 = loc, dst - phase
            n_chunks = jnp.where(n > 0, (phase + n + (PERM_GRAN - 1)) // PERM_GRAN, 0)
        else:
            to_aligned = (PERM_GRAN - phase) & (PERM_GRAN - 1)
            n_head = jnp.minimum(n, to_aligned)
            loc0 = loc + jnp.where(phase > 0, PERM_GRAN, 0)
            dst0 = dst + to_aligned
            n_chunks = (jnp.maximum(n - to_aligned, 0) + (PERM_GRAN - 1)) // PERM_GRAN

            def head(j, carry, loc=loc, dst=dst, phase=phase):
                row_fn(loc + phase + j, dst + j)
                return carry

            lax.fori_loop(0, n_head, head, 0)

        def body(c, carry, loc0=loc0, dst0=dst0):
            chunk_fn(pl.multiple_of(loc0 + c * PERM_GRAN, PERM_GRAN), pl.multiple_of(dst0 + c * PERM_GRAN, PERM_GRAN))
            return carry

        lax.fori_loop(0, n_chunks, body, 0)
        chunks = chunks + n_chunks
        rows = rows + n_head
    return chunks, rows


def _dispatch_kernel(pend_ref, pcnt_ref, cnt_ref, off_ref, dst_ref, h_ref, pos_ref, xs_ref,
                     xc_ref, zbuf_ref, tot_ref, sem, zsem, *, block_rows):
    i = pl.program_id(0)
    slot = i % 2

    def fill(first_row):
        start = pl.multiple_of(first_row, block_rows)
        return pltpu.make_async_copy(zbuf_ref, xs_ref.at[pl.ds(start, block_rows)], zsem)

    @pl.when(i == 0)
    def _():
        zbuf_ref[...] = jnp.zeros_like(zbuf_ref)
        used_blocks = pend_ref[N_EXPERTS - 1] // block_rows
        all_blocks = xs_ref.shape[0] // block_rows
        for e in range(N_EXPERTS):
            @pl.when(pcnt_ref[e] > 0)
            def _():
                fill(pend_ref[e] - block_rows).start()
        lax.fori_loop(used_blocks, all_blocks, lambda j, c: (fill(j * block_rows).start(), c)[1], 0)
        for e in range(N_EXPERTS):
            @pl.when(pcnt_ref[e] > 0)
            def _():
                fill(pend_ref[e] - block_rows).wait()
        lax.fori_loop(used_blocks, all_blocks, lambda j, c: (fill(j * block_rows).wait(), c)[1], 0)

    pos = pos_ref[...]
    lane = lax.broadcasted_iota(I32, (pos.shape[0], PERM_LOCAL), 1)
    pt = jnp.where(lane == pos[:, 0:1], 1.0, 0.0)
    for k in range(1, TOP_K):
        pt = pt + jnp.where(lane == pos[:, k:k + 1], 1.0, 0.0)
    ptb = pt.astype(BF16)
    hb = h_ref[...].astype(BF16)
    half = hb.shape[1] // 2
    xc_ref[slot] = _pack_bf16_pair(_dot_tn(ptb, hb[:, :half]), _dot_tn(ptb, hb[:, half:]))

    def copy(s, local_row, global_row, size):
        return pltpu.make_async_copy(xc_ref.at[s, pl.ds(local_row, size)],
                                     xs_ref.at[pl.ds(global_row, size)], sem.at[s])

    def drain(s):
        lax.fori_loop(0, tot_ref[0], lambda j, c: (copy(s, 0, 0, PERM_GRAN).wait(), c)[1], 0)
        lax.fori_loop(0, tot_ref[1], lambda j, c: (copy(s, 0, 0, 1).wait(), c)[1], 0)

    @pl.when(i > 0)
    def _():
        drain(1 - slot)

    chunks, heads = _for_each_run(cnt_ref, off_ref, dst_ref, i % TABLE_GROUP,
                                  lambda lr, gr: copy(slot, lr, gr, PERM_GRAN).start(),
                                  lambda lr, gr: copy(slot, lr, gr, 1).start())
    tot_ref[0] = chunks
    tot_ref[1] = heads

    @pl.when(i == pl.num_programs(0) - 1)
    def _():
        drain(slot)


def _table_spec(shift, nblk):
    return pl.BlockSpec((SMEM_TABLE,), lambda i, *_: (jnp.minimum(i + shift, nblk - 1) // TABLE_GROUP,),
                        memory_space=pltpu.SMEM)


def _dispatch(h, pos, cnt_tbl, off_tbl, dst_tbl, pend, pcnt, total_rows, layer):
    n, d = h.shape
    rows = min(PERM_ROWS, n)
    nblk = n // rows
    return pl.pallas_call(
        functools.partial(_dispatch_kernel, block_rows=EXPERT_ROWS),
        grid_spec=pltpu.PrefetchScalarGridSpec(
            num_scalar_prefetch=2,
            grid=(nblk,),
            in_specs=[_table_spec(0, nblk), _table_spec(0, nblk), _table_spec(0, nblk),
                      pl.BlockSpec((rows, d), lambda i, *_: (i, 0)),
                      pl.BlockSpec((rows, TOP_K), lambda i, *_: (i, 0))],
            out_specs=pl.BlockSpec(memory_space=pl.ANY),
            scratch_shapes=[pltpu.VMEM((2, PERM_LOCAL, d // 2), U32), pltpu.VMEM((EXPERT_ROWS, d // 2), U32),
                            pltpu.SMEM((2,), I32), pltpu.SemaphoreType.DMA((2,)), pltpu.SemaphoreType.DMA(())],
        ),
        out_shape=jax.ShapeDtypeStruct((total_rows, d // 2), U32),
        compiler_params=pltpu.CompilerParams(dimension_semantics=("arbitrary",), vmem_limit_bytes=VMEM_LIMIT_BYTES),
        name=f"dispatch_l{layer}",
    )(pend, pcnt, cnt_tbl, off_tbl, dst_tbl, h, pos)


def _combine_kernel(cnt_ref, off_ref, dst_ref, ncnt_ref, noff_ref, ndst_ref, h_ref, gate_ref, pos_ref,
                    lg_ref, lb_ref, ys_ref, o_ref, yc_ref, tot_ref, sem, *, alpha):
    i = pl.program_id(0)
    slot = i % 2

    def chunk_copy(s, local_row, global_row):
        return pltpu.make_async_copy(ys_ref.at[pl.ds(global_row, PERM_GRAN)],
                                     yc_ref.at[s, pl.ds(local_row, PERM_GRAN)], sem.at[s])

    @pl.when(i == 0)
    def _():
        yc_ref[...] = jnp.zeros_like(yc_ref)
        tot_ref[0] = _for_each_run(cnt_ref, off_ref, dst_ref, 0, lambda lr, gr: chunk_copy(0, lr, gr).start())[0]

    @pl.when(i + 1 < pl.num_programs(0))
    def _():
        tot_ref[1 - slot] = _for_each_run(ncnt_ref, noff_ref, ndst_ref, (i + 1) % TABLE_GROUP,
                                          lambda lr, gr: chunk_copy(1 - slot, lr, gr).start())[0]

    lax.fori_loop(0, tot_ref[slot], lambda j, c: (chunk_copy(slot, 0, 0).wait(), c)[1], 0)

    pos = pos_ref[...]
    gates = gate_ref[...]
    lane = lax.broadcasted_iota(I32, (pos.shape[0], PERM_LOCAL), 1)
    q = jnp.where(lane == pos[:, 0:1], gates[:, 0:1], 0.0)
    for k in range(1, TOP_K):
        q = q + jnp.where(lane == pos[:, k:k + 1], gates[:, k:k + 1], 0.0)
    qb = q.astype(BF16)
    y_lo, y_hi = _unpack_bf16_pair(yc_ref[slot])
    ffn = jnp.concatenate([_dot(qb, y_lo), _dot(qb, y_hi)], axis=-1)
    o_ref[...] = _layer_norm(alpha * h_ref[...] + ffn, lg_ref[...], lb_ref[...])


def _combine(h, gates, pos, cnt_tbl, off_tbl, dst_tbl, ys, ln_g, ln_b, layer, alpha):
    n, d = h.shape
    rows = min(PERM_ROWS, n)
    nblk = n // rows
    return pl.pallas_call(
        functools.partial(_combine_kernel, alpha=alpha),
        grid=(nblk,),
        in_specs=[_table_spec(0, nblk), _table_spec(0, nblk), _table_spec(0, nblk),
                  _table_spec(1, nblk), _table_spec(1, nblk), _table_spec(1, nblk),
                  pl.BlockSpec((rows, d), lambda i: (i, 0)),
                  pl.BlockSpec((rows, TOP_K), lambda i: (i, 0)),
                  pl.BlockSpec((rows, TOP_K), lambda i: (i, 0)),
                  pl.BlockSpec((None, 1, d), lambda i: (layer, 0, 0)),
                  pl.BlockSpec((None, 1, d), lambda i: (layer, 0, 0)),
                  pl.BlockSpec(memory_space=pl.ANY)],
        out_specs=pl.BlockSpec((rows, d), lambda i: (i, 0)),
        out_shape=jax.ShapeDtypeStruct((n, d), F32),
        scratch_shapes=[pltpu.VMEM((2, PERM_LOCAL, d // 2), U32), pltpu.SMEM((2,), I32),
                        pltpu.SemaphoreType.DMA((2,))],
        compiler_params=pltpu.CompilerParams(dimension_semantics=("arbitrary",), vmem_limit_bytes=VMEM_LIMIT_BYTES),
        name=f"combine_l{layer}",
    )(cnt_tbl, off_tbl, dst_tbl, cnt_tbl, off_tbl, dst_tbl, h, gates, pos, ln_g, ln_b, ys)


def _expert_kernel(be_ref, nact_ref, x_ref, wgu_ref, bgu_ref, wd_ref, bd_ref, y_ref):
    d_ff = wd_ref.shape[0]
    half = y_ref.shape[1]

    @pl.when(pl.program_id(0) < nact_ref[0])
    def _():
        x_lo, x_hi = _unpack_bf16_pair(x_ref[...])
        gu = _dot(x_lo, wgu_ref[0:half, :]) + _dot(x_hi, wgu_ref[half:2 * half, :]) + bgu_ref[...]
        gate = jnp.minimum(gu[:, :d_ff], SWIGLU_LIMIT)
        up = jnp.clip(gu[:, d_ff:], -SWIGLU_LIMIT, SWIGLU_LIMIT)
        act = (up + 1.0) * gate * _sigmoid(SWIGLU_ALPHA * gate)
        y = _dot(act.astype(BF16), wd_ref[...]) + bd_ref[...]
        y_ref[...] = _pack_bf16_pair(_round_bf16(y[:, :half]), _round_bf16(y[:, half:]))

    @pl.when(pl.program_id(0) >= nact_ref[0])
    def _():
        y_ref[...] = jnp.zeros_like(y_ref)


def _experts(xs, blk_expert, nact, wgu, bgu, wd, bd, layer):
    total_rows, half = xs.shape
    nblk = total_rows // EXPERT_ROWS
    d = wgu.shape[2]
    d_ff = wd.shape[2]

    def row_block(j, be, na):
        return (j, 0)

    def of_expert(j, be, na):
        return (layer, be[j], 0, 0)

    return pl.pallas_call(
        _expert_kernel,
        grid_spec=pltpu.PrefetchScalarGridSpec(
            num_scalar_prefetch=2,
            grid=(nblk,),
            in_specs=[pl.BlockSpec((EXPERT_ROWS, half), row_block),
                      pl.BlockSpec((None, None, d, 2 * d_ff), of_expert),
                      pl.BlockSpec((None, None, 1, 2 * d_ff), of_expert),
                      pl.BlockSpec((None, None, d_ff, d), of_expert),
                      pl.BlockSpec((None, None, 1, d), of_expert)],
            out_specs=pl.BlockSpec((EXPERT_ROWS, half), row_block),
        ),
        out_shape=jax.ShapeDtypeStruct((total_rows, half), U32),
        compiler_params=pltpu.CompilerParams(dimension_semantics=("arbitrary",), vmem_limit_bytes=VMEM_LIMIT_BYTES),
        name=f"experts_l{layer}",
    )(blk_expert, nact, xs, wgu, bgu, wd, bd)


def _moe(h, w_router_p, b_router_p, wgu, bgu, wd, bd, ln_g, ln_b, layer, alpha):
    n, d = h.shape
    gates, pos, cnt_tbl, off_tbl, before_tbl, totals = _router(h, w_router_p, b_router_p, layer)
    counts = totals[0, :N_EXPERTS]
    padded = (counts + PERM_GRAN + EXPERT_ROWS - 1) // EXPERT_ROWS * EXPERT_ROWS
    padded = jnp.where(counts > 0, padded, 0)
    pend = jnp.cumsum(padded).astype(I32)
    pstart = pend - padded
    nblk = (n * TOP_K + N_EXPERTS * PERM_GRAN) // EXPERT_ROWS + N_EXPERTS
    nact = (pend[-1] // EXPERT_ROWS).reshape(1).astype(I32)
    blk = jnp.minimum(jnp.arange(nblk, dtype=I32), nact[0] - 1) * EXPERT_ROWS
    blk_expert = jnp.minimum(jnp.sum(pend[None, :] <= blk[:, None], axis=1), N_EXPERTS - 1).astype(I32)
    start_row = jnp.zeros((LANES,), I32).at[:N_EXPERTS].set(pstart)

    def flat(tbl):
        tbl = tbl.reshape(-1)
        return jnp.pad(tbl, (0, -tbl.shape[0] % SMEM_TABLE))

    cnt_f, off_f = flat(cnt_tbl), flat(off_tbl)
    dst_f = flat(before_tbl + start_row[None, None, :])

    xs = _dispatch(h, pos, cnt_f, off_f, dst_f, pend, padded.astype(I32), nblk * EXPERT_ROWS, layer)
    ys = _experts(xs, blk_expert, nact, wgu, bgu, wd, bd, layer)
    return _combine(h, gates, pos, cnt_f, off_f, dst_f, ys, ln_g, ln_b, layer, alpha)


def _relayout_w_in(w_in):
    depth, d, _ = w_in.shape
    out = jnp.zeros((depth, d, D_IN_PAD), BF16)
    for g in range(len(IN_SPLITS)):
        lo, hi = _COLS[g]
        out = out.at[:, :, lo:hi].set(w_in[:, :, _SRC_OFF[g]:_SRC_OFF[g] + IN_SPLITS[g]].astype(BF16))
    return out


def kernel(x, ln_in_g, ln_in_b, w_in, w_gla_gate_expand, b_gla_gate, gla_norm_g, hgrn_lb_raw, hgrn_norm_g, conv_w, w_out, ln_mix_g, ln_mix_b, w_router, b_router, w_gate_up, b_gate_up, w_down, b_down, ln_ffn_g, ln_ffn_b):
    bsz, t, d = x.shape
    depth = w_in.shape[0]
    alpha = (2 * depth) ** 0.25
    n = bsz * t

    w_in_r = _relayout_w_in(w_in)
    w_exp = jnp.zeros((depth, LANES, GLA_K_DIM), F32).at[:, :GLA_GATE_RANK, :].set(w_gla_gate_expand)
    b_gate = b_gla_gate.reshape(depth, 1, GLA_K_DIM)
    gla_g = jnp.tile(gla_norm_g, (1, GLA_HEADS)).reshape(depth, 1, GLA_V_DIM)
    hgrn_g = jnp.tile(hgrn_norm_g, (1, HGRN_HEADS)).reshape(depth, 1, HGRN_I_DIM)
    w_out_b = w_out.astype(BF16)
    w_router_p = jnp.zeros((depth, d, LANES), F32).at[:, :, :N_EXPERTS].set(w_router)
    b_router_p = jnp.zeros((depth, 1, LANES), F32).at[:, 0, :N_EXPERTS].set(b_router)
    wgu = w_gate_up.astype(BF16)
    wd = w_down.astype(BF16)
    bgu = b_gate_up.reshape(depth, N_EXPERTS, 1, -1)
    bd = b_down.reshape(depth, N_EXPERTS, 1, -1)
    row = lambda a: a.reshape(depth, 1, d)
    consts = _mixer_consts(min(MIX_ROWS, t))

    h = x
    for l in range(depth):
        h = _mixer(h, ln_in_g.reshape(1, d), ln_in_b.reshape(1, d), w_in_r, w_exp, b_gate, gla_g, hgrn_lb_raw,
                   hgrn_g, conv_w, w_out_b, row(ln_mix_g), row(ln_mix_b), consts,
                   layer=l, pre_ln=(l == 0), alpha=alpha)
        h = _moe(h.reshape(n, d), w_router_p, b_router_p, wgu, bgu, wd, bd, row(ln_ffn_g), row(ln_ffn_b),
                 l, alpha).reshape(bsz, t, d)
    return h
```

```python
import functools

import jax
import jax.numpy as jnp
from jax import lax
from jax.experimental import pallas as pl
from jax.experimental.pallas import tpu as pltpu

F32 = jnp.float32
BF16 = jnp.bfloat16
U32 = jnp.uint32
I32 = jnp.int32

GLA_HEADS = 4
GLA_HEAD_K = 64
GLA_HEAD_V = 128
GLA_GATE_RANK = 16
GLA_GATE_NORMALIZER = 16.0
HGRN_HEADS = 4
HGRN_HEAD_F = 64
HGRN_HEAD_I = 64
CONV_CHANNELS = 256
CONV_WIDTH = 3
CHUNK = 64
N_EXPERTS = 32
TOP_K = 4
SWIGLU_LIMIT = 7.0
SWIGLU_ALPHA = 1.702
LN_EPS = 1e-5
RMS_EPS = 1e-6

GLA_K_DIM = GLA_HEADS * GLA_HEAD_K
GLA_V_DIM = GLA_HEADS * GLA_HEAD_V
HGRN_F_DIM = HGRN_HEADS * HGRN_HEAD_F
HGRN_I_DIM = HGRN_HEADS * HGRN_HEAD_I
IN_SPLITS = (GLA_K_DIM, GLA_K_DIM, GLA_V_DIM, GLA_GATE_RANK, GLA_V_DIM,
             HGRN_F_DIM, HGRN_F_DIM, HGRN_I_DIM, HGRN_I_DIM,
             CONV_CHANNELS, CONV_CHANNELS, CONV_CHANNELS)

LANES = 128
SUBLANES = 8
VMEM_LIMIT_BYTES = 56 * 1024 * 1024

MIX_ROWS = 512
CUMSUM_ROWS = 256
PERM_ROWS = 256
PERM_GRAN = SUBLANES
PERM_CHUNK = 16
PERM_LOCAL = -(-(PERM_ROWS * TOP_K + N_EXPERTS * (PERM_GRAN + PERM_CHUNK - 2)) // LANES) * LANES
EXPERT_ROWS = 512
SMEM_TABLE = 1024
TABLE_GROUP = SMEM_TABLE // LANES

_ORDER = (0, 1, 2, 4, 5, 6, 7, 8, 9, 10, 11, 3)
_SRC_OFF = [sum(IN_SPLITS[:i]) for i in range(len(IN_SPLITS))]
_COLS = {}
_off = 0
for _g in _ORDER:
    _w = IN_SPLITS[_g]
    _COLS[_g] = (_off, _off + _w)
    _off += -(-_w // LANES) * LANES
D_IN_PAD = _off
(C_GQ, C_GK, C_GV, C_GLR, C_GOG, C_HQ, C_HF, C_HI, C_HOG, C_CB, C_CC, C_CH) = (_COLS[i] for i in range(12))


def _layer_norm(x, g, b):
    mu = jnp.mean(x, axis=-1, keepdims=True)
    xc = x - mu
    var = jnp.mean(xc * xc, axis=-1, keepdims=True)
    return xc * lax.rsqrt(var + LN_EPS) * g + b


def _sigmoid(x):
    return 1.0 / (1.0 + jnp.exp(-x))


def _dot(a, b):
    return jnp.dot(a, b, preferred_element_type=F32)


def _dot_nt(a, b):
    return lax.dot_general(a, b, (((1,), (1,)), ((), ())), preferred_element_type=F32)


def _dot_tn(a, b):
    return lax.dot_general(a, b, (((0,), (0,)), ((), ())), preferred_element_type=F32)


def _dot_f32(a, b):
    return lax.dot_general(a, b, (((1,), (0,)), ((), ())), precision=lax.Precision.HIGHEST,
                           preferred_element_type=F32)


def _split_dot(a01, x):
    hi = x.astype(BF16)
    lo = (x - hi.astype(F32)).astype(BF16)
    return _dot(a01, hi) + _dot(a01, lo)


def _columns_to_lanes(cols, width):
    lane = lax.broadcasted_iota(I32, (cols[0].shape[0], width), 1)
    out = jnp.broadcast_to(cols[-1], lane.shape)
    for k in range(width - 2, -1, -1):
        out = jnp.where(lane == k, cols[k], out)
    return out


def _pack_bf16_pair(lo, hi):
    return lax.bitcast_convert_type(hi, U32) | (lax.bitcast_convert_type(lo, U32) >> 16)


def _unpack_bf16_pair(word):
    lo = lax.bitcast_convert_type(word << 16, F32).astype(BF16)
    hi = lax.bitcast_convert_type(word & jnp.uint32(0xFFFF0000), F32).astype(BF16)
    return lo, hi


def _round_bf16(x):
    return x.astype(BF16).astype(F32)


def _chunk_recurrence(q, k, v, g, st_ref, tril, causal, bmask, heads, dk, dv):
    rows = q.shape[0]
    cum = tril.shape[0]
    gc = jnp.concatenate([_split_dot(tril, g[r:r + cum]) for r in range(0, rows, cum)], axis=0)
    lane_k = lax.broadcasted_iota(I32, (1, heads * dk), 1) // dk
    lane_v = lax.broadcasted_iota(I32, (1, heads * dv), 1) // dv
    st = st_ref[...]
    outs = []
    for c in range(rows // CHUNK):
        sl = slice(c * CHUNK, (c + 1) * CHUNK)
        gcc, qc, kc, vc = gc[sl], q[sl], k[sl], v[sl]
        mid = gcc[CHUNK // 2 - 1:CHUNK // 2]
        last = gcc[CHUNK - 1:CHUNK]
        qt = qc * jnp.exp(gcc - mid)
        kt = kc * jnp.exp(mid - gcc)
        qi = qc * jnp.exp(gcc)
        ks = kc * jnp.exp(last - gcc)
        vb = vc.astype(BF16)
        qstack = jnp.concatenate([jnp.where(lane_k == h, qt, 0.0) for h in range(heads)], axis=0)
        a = _dot_nt(qstack.astype(BF16), kt.astype(BF16))
        a = jnp.where(causal, a, 0.0)
        o_all = _dot(a.astype(BF16), vb)
        o_intra = jnp.where(lane_v == 0, o_all[0:CHUNK], 0.0)
        for h in range(1, heads):
            o_intra = o_intra + jnp.where(lane_v == h, o_all[h * CHUNK:(h + 1) * CHUNK], 0.0)
        o_inter = _dot_nt(qi.astype(BF16), st.astype(BF16))
        u = _dot_tn(vb, ks.astype(BF16))
        st = jnp.exp(last) * st + u * bmask
        outs.append(o_intra + o_inter)
    st_ref[...] = st
    return jnp.concatenate(outs, axis=0)


def _gated_head_rmsnorm(o, gate, g_tiled, heads, dv):
    lane_v = lax.broadcasted_iota(I32, (1, heads * dv), 1) // dv
    sq = o * o
    ms = jnp.zeros_like(o)
    for h in range(heads):
        m = lane_v == h
        ms = jnp.where(m, jnp.sum(jnp.where(m, sq, 0.0), axis=-1, keepdims=True), ms)
    ms = ms * (1.0 / dv)
    return o * lax.rsqrt(ms + RMS_EPS) * g_tiled * (gate * _sigmoid(gate))


def _mixer_kernel(x_ref, lng_ref, lnb_ref, win_ref, wexp_ref, bgate_ref, glag_ref, lbraw_ref, hgg_ref,
                  convw_ref, wout_ref, mg_ref, mb_ref, tril_ref, causal_ref, bmg_ref, bmh_ref,
                  o_ref, sg_ref, sh_ref, cc_ref, *, layer, pre_ln, alpha):
    @pl.when(pl.program_id(1) == 0)
    def _():
        sg_ref[...] = jnp.zeros_like(sg_ref)
        sh_ref[...] = jnp.zeros_like(sh_ref)
        cc_ref[...] = jnp.zeros_like(cc_ref)

    x = x_ref[...]
    h = _layer_norm(x, lng_ref[...], lnb_ref[...]) if pre_ln else x
    hb = h.astype(BF16)
    rows = x.shape[0]

    def proj(cols):
        return _dot(hb, win_ref[:, cols[0]:cols[0] + -(-(cols[1] - cols[0]) // LANES) * LANES])

    tril = tril_ref[...]
    causal = causal_ref[...] > 0.5

    q = proj(C_GQ) * (GLA_HEAD_K ** -0.5)
    gate_logits = _dot_f32(proj(C_GLR), wexp_ref[...]) + bgate_ref[...]
    log_a = (jnp.minimum(gate_logits, 0.0) - jnp.log(1.0 + jnp.exp(-jnp.abs(gate_logits)))) * (1.0 / GLA_GATE_NORMALIZER)
    o_gla = _chunk_recurrence(q, proj(C_GK), proj(C_GV), log_a, sg_ref, tril, causal, bmg_ref[...],
                              GLA_HEADS, GLA_HEAD_K, GLA_HEAD_V)
    y_gla = _gated_head_rmsnorm(o_gla, proj(C_GOG), glag_ref[...], GLA_HEADS, GLA_HEAD_V)

    raw = lbraw_ref[...]
    e = jnp.exp(raw - jnp.max(raw, axis=0, keepdims=True))
    lb = jnp.zeros_like(e[0:1])
    for i in range(1, layer + 1):
        lb = lb + e[i:i + 1]
    lb = lb / jnp.sum(e, axis=0, keepdims=True)
    f = lb + (1.0 - lb) * _sigmoid(proj(C_HF))
    hq = proj(C_HQ)
    o_h = _chunk_recurrence(hq * _sigmoid(hq), 1.0 - f, proj(C_HI), jnp.log(f), sh_ref, tril, causal,
                            bmh_ref[...], HGRN_HEADS, HGRN_HEAD_F, HGRN_HEAD_I)
    y_h = _gated_head_rmsnorm(o_h, proj(C_HOG), hgg_ref[...], HGRN_HEADS, HGRN_HEAD_I)

    u = proj(C_CC) * proj(C_CH)
    prev = cc_ref[...]
    row = lax.broadcasted_iota(I32, (rows, 1), 0)
    u1 = jnp.where(row == 0, prev[SUBLANES - 1:SUBLANES], pltpu.roll(u, 1, 0))
    u2 = jnp.where(row == 0, prev[SUBLANES - 2:SUBLANES - 1],
                   jnp.where(row == 1, prev[SUBLANES - 1:SUBLANES], pltpu.roll(u, 2, 0)))
    cc_ref[...] = u[rows - SUBLANES:rows]
    cw = convw_ref[...]
    y_conv = proj(C_CB) * (cw[0:1] * u2 + cw[1:2] * u1 + cw[2:3] * u)

    o1, o2 = GLA_V_DIM, GLA_V_DIM + HGRN_I_DIM
    mix = (_dot(y_gla.astype(BF16), wout_ref[0:o1, :]) + _dot(y_h.astype(BF16), wout_ref[o1:o2, :])
           + _dot(y_conv.astype(BF16), wout_ref[o2:o2 + CONV_CHANNELS, :]))
    o_ref[...] = _layer_norm(alpha * h + mix, mg_ref[...], mb_ref[...])


def _mixer(x, ln_g, ln_b, w_in_r, w_exp, b_gate, gla_g, lb_raw, hgrn_g, conv_w, w_out, mg, mb,
           consts, *, layer, pre_ln, alpha):
    bsz, t, d = x.shape
    rows = min(MIX_ROWS, t)
    tril, causal, bmg, bmh = consts
    assert t % rows == 0 and rows % tril.shape[0] == 0 and tril.shape[0] % CHUNK == 0

    def lay(shape):
        return pl.BlockSpec((None,) + shape, lambda b, i: (layer,) + (0,) * len(shape))

    def whole(a):
        return pl.BlockSpec(a.shape, lambda b, i: (0,) * a.ndim)

    return pl.pallas_call(
        functools.partial(_mixer_kernel, layer=layer, pre_ln=pre_ln, alpha=alpha),
        grid=(bsz, t // rows),
        in_specs=[
            pl.BlockSpec((None, rows, d), lambda b, i: (b, i, 0)),
            whole(ln_g), whole(ln_b),
            lay((d, D_IN_PAD)), lay((LANES, GLA_K_DIM)), lay((1, GLA_K_DIM)), lay((1, GLA_V_DIM)),
            whole(lb_raw), lay((1, HGRN_I_DIM)), lay((CONV_WIDTH, CONV_CHANNELS)), lay((d, d)),
            lay((1, d)), lay((1, d)),
            whole(tril), whole(causal), whole(bmg), whole(bmh),
        ],
        out_specs=pl.BlockSpec((None, rows, d), lambda b, i: (b, i, 0)),
        out_shape=jax.ShapeDtypeStruct((bsz, t, d), F32),
        scratch_shapes=[pltpu.VMEM((GLA_V_DIM, GLA_K_DIM), F32), pltpu.VMEM((HGRN_I_DIM, HGRN_F_DIM), F32),
                        pltpu.VMEM((SUBLANES, CONV_CHANNELS), F32)],
        compiler_params=pltpu.CompilerParams(dimension_semantics=("arbitrary", "arbitrary"),
                                             vmem_limit_bytes=VMEM_LIMIT_BYTES),
        name=f"mixer_l{layer}",
    )(x, ln_g, ln_b, w_in_r, w_exp, b_gate, gla_g, lb_raw, hgrn_g, conv_w, w_out, mg, mb, tril, causal, bmg, bmh)


def _mixer_consts(rows):
    r = jnp.arange(rows)
    tril = ((r[:, None] // CHUNK == r[None, :] // CHUNK) & (r[None, :] <= r[:, None])).astype(BF16)
    i = jnp.arange(GLA_HEADS * CHUNK) % CHUNK
    causal = (jnp.arange(CHUNK)[None, :] <= i[:, None]).astype(F32)
    bmg = (jnp.arange(GLA_V_DIM)[:, None] // GLA_HEAD_V == jnp.arange(GLA_K_DIM)[None, :] // GLA_HEAD_K).astype(F32)
    bmh = (jnp.arange(HGRN_I_DIM)[:, None] // HGRN_HEAD_I == jnp.arange(HGRN_F_DIM)[None, :] // HGRN_HEAD_F).astype(F32)
    return tril, causal, bmg, bmh


def _router_kernel(h_ref, wr_ref, br_ref, ls_ref, ut_ref, gate_ref, pos_ref, cnt_ref, off_ref, before_ref,
                   total_ref, carry_ref):
    @pl.when(pl.program_id(0) == 0)
    def _():
        carry_ref[...] = jnp.zeros_like(carry_ref)

    logits = _dot_f32(h_ref[...], wr_ref[...]) + br_ref[...]
    lane = lax.broadcasted_iota(I32, logits.shape, 1).astype(F32)
    neg = jnp.float32(-jnp.inf)
    l = jnp.where(lane < N_EXPERTS, logits, neg)
    tops, idxs = [], []
    for _ in range(TOP_K):
        m = jnp.max(l, axis=-1, keepdims=True)
        ik = jnp.min(jnp.where(l == m, lane, float(LANES)), axis=-1, keepdims=True)
        tops.append(m)
        idxs.append(ik)
        l = jnp.where(lane == ik, neg, l)
    es = [jnp.exp(t - tops[0]) for t in tops]
    denom = es[0] + es[1] + es[2] + es[3]
    onehot = jnp.zeros_like(logits)
    for ik in idxs:
        onehot = onehot + jnp.where(lane == ik, 1.0, 0.0)

    carry = carry_ref[...]
    earlier = _dot(ls_ref[...], onehot.astype(BF16))
    cnt = jnp.sum(onehot, axis=0, keepdims=True)
    phase = carry - PERM_GRAN * jnp.floor(carry * (1.0 / PERM_GRAN))
    run = jnp.where(cnt > 0.0, jnp.ceil((phase + cnt) * (1.0 / PERM_CHUNK)) * PERM_CHUNK, 0.0)
    off = _dot_f32(jnp.broadcast_to(run, (SUBLANES, LANES)), ut_ref[...])[0:1]
    local = earlier + (off + phase)
    pos = [jnp.sum(jnp.where(lane == ik, local, 0.0), axis=-1, keepdims=True) for ik in idxs]

    cnt_ref[...] = cnt.astype(I32)
    off_ref[...] = off.astype(I32)
    before_ref[...] = carry.astype(I32)
    carry = carry + cnt
    carry_ref[...] = carry
    total_ref[...] = carry.astype(I32)
    gate_ref[...] = _columns_to_lanes([e / denom for e in es], TOP_K)
    pos_ref[...] = _columns_to_lanes(pos, TOP_K).astype(I32)


def _router(h, w_router_p, b_router_p, layer):
    n, d = h.shape
    rows = min(PERM_ROWS, n)
    assert n % rows == 0
    nblk = n // rows
    r = jnp.arange(rows)
    lstrict = (r[None, :] < r[:, None]).astype(BF16)
    e = jnp.arange(LANES)
    upper = (e[:, None] < e[None, :]).astype(F32)
    pair = pl.BlockSpec((rows, TOP_K), lambda i: (i, 0))
    table = pl.BlockSpec((None, 1, LANES), lambda i: (i, 0, 0))
    table_shape = jax.ShapeDtypeStruct((nblk, 1, LANES), I32)
    return pl.pallas_call(
        _router_kernel,
        grid=(nblk,),
        in_specs=[pl.BlockSpec((rows, d), lambda i: (i, 0)),
                  pl.BlockSpec((None, d, LANES), lambda i: (layer, 0, 0)),
                  pl.BlockSpec((None, 1, LANES), lambda i: (layer, 0, 0)),
                  pl.BlockSpec((rows, rows), lambda i: (0, 0)),
                  pl.BlockSpec((LANES, LANES), lambda i: (0, 0))],
        out_specs=[pair, pair, table, table, table, pl.BlockSpec((1, LANES), lambda i: (0, 0))],
        out_shape=[jax.ShapeDtypeStruct((n, TOP_K), F32), jax.ShapeDtypeStruct((n, TOP_K), I32),
                   table_shape, table_shape, table_shape, jax.ShapeDtypeStruct((1, LANES), I32)],
        scratch_shapes=[pltpu.VMEM((1, LANES), F32)],
        compiler_params=pltpu.CompilerParams(dimension_semantics=("arbitrary",), vmem_limit_bytes=VMEM_LIMIT_BYTES),
        name=f"router_l{layer}",
    )(h, w_router_p, b_router_p, lstrict, upper)


def _for_each_run(cnt_ref, off_ref, dst_ref, table_row, chunk_fn, run_fn=None):
    base = table_row * LANES
    chunks = jnp.int32(0)
    for e in range(N_EXPERTS):
        n = cnt_ref[base + e]
        loc = off_ref[base + e]
        dst = dst_ref[base + e]
        phase = dst & (PERM_GRAN - 1)
        if run_fn is not None:
            run_fn(e, n, loc, phase)
        dst0 = dst - phase
        n_chunks = jnp.where(n > 0, (phase + n + (PERM_CHUNK - 1)) // PERM_CHUNK, 0)

        def body(c, carry, loc=loc, dst0=dst0):
            chunk_fn(pl.multiple_of(loc + c * PERM_CHUNK, PERM_CHUNK), pl.multiple_of(dst0 + c * PERM_CHUNK, PERM_GRAN))
            return carry

        lax.fori_loop(0, n_chunks, body, 0)
        chunks = chunks + n_chunks
    return chunks


def _dispatch_kernel(pend_ref, pcnt_ref, cnt_ref, off_ref, dst_ref, h_ref, pos_ref, xs_ref,
                     xc_ref, zbuf_ref, tail_ref, tot_ref, sem, zsem, *, block_rows):
    i = pl.program_id(0)
    slot = i % 2

    def fill(first_row):
        start = pl.multiple_of(first_row, block_rows)
        return pltpu.make_async_copy(zbuf_ref, xs_ref.at[pl.ds(start, block_rows)], zsem)

    @pl.when(i == 0)
    def _():
        zbuf_ref[...] = jnp.zeros_like(zbuf_ref)
        tail_ref[...] = jnp.zeros_like(tail_ref)
        used_blocks = pend_ref[N_EXPERTS - 1] // block_rows
        all_blocks = xs_ref.shape[0] // block_rows
        for e in range(N_EXPERTS):
            @pl.when(pcnt_ref[e] > 0)
            def _():
                fill(pend_ref[e] - block_rows).start()
        lax.fori_loop(used_blocks, all_blocks, lambda j, c: (fill(j * block_rows).start(), c)[1], 0)
        for e in range(N_EXPERTS):
            @pl.when(pcnt_ref[e] > 0)
            def _():
                fill(pend_ref[e] - block_rows).wait()
        lax.fori_loop(used_blocks, all_blocks, lambda j, c: (fill(j * block_rows).wait(), c)[1], 0)

    pos = pos_ref[...]
    lane = lax.broadcasted_iota(I32, (pos.shape[0], PERM_LOCAL), 1)
    pt = jnp.where(lane == pos[:, 0:1], 1.0, 0.0)
    for k in range(1, TOP_K):
        pt = pt + jnp.where(lane == pos[:, k:k + 1], 1.0, 0.0)
    ptb = pt.astype(BF16)
    hb = h_ref[...].astype(BF16)
    half = hb.shape[1] // 2
    xc_ref[slot] = _pack_bf16_pair(_dot_tn(ptb, hb[:, :half]), _dot_tn(ptb, hb[:, half:]))

    def copy(s, local_row, global_row, size):
        return pltpu.make_async_copy(xc_ref.at[s, pl.ds(local_row, size)],
                                     xs_ref.at[pl.ds(global_row, size)], sem.at[s])

    def drain(s):
        @pl.when(tot_ref[0] > 0)
        def _():
            copy(s, 0, 0, tot_ref[0] * PERM_CHUNK).wait()

    @pl.when(i > 0)
    def _():
        drain(1 - slot)

    row8 = lax.broadcasted_iota(I32, (PERM_GRAN, 1), 0)

    def carry_partial_group(e, n, loc, phase):
        first = pl.ds(pl.multiple_of(loc, PERM_GRAN), PERM_GRAN)
        merged = jnp.where((row8 < phase) & (n > 0), tail_ref[e], xc_ref[slot, first, :])
        xc_ref[slot, first, :] = merged
        end = pl.ds(pl.multiple_of(loc + (phase + n) // PERM_GRAN * PERM_GRAN, PERM_GRAN), PERM_GRAN)
        tail_ref[e] = jnp.where(n > 0, xc_ref[slot, end, :], tail_ref[e])

    tot_ref[0] = _for_each_run(cnt_ref, off_ref, dst_ref, i % TABLE_GROUP,
                               lambda lr, gr: copy(slot, lr, gr, PERM_CHUNK).start(), carry_partial_group)

    @pl.when(i == pl.num_programs(0) - 1)
    def _():
        drain(slot)


def _table_spec(shift, nblk):
    return pl.BlockSpec((SMEM_TABLE,), lambda i, *_: (jnp.minimum(i + shift, nblk - 1) // TABLE_GROUP,),
                        memory_space=pltpu.SMEM)


def _dispatch(h, pos, cnt_tbl, off_tbl, dst_tbl, pend, pcnt, total_rows, layer):
    n, d = h.shape
    rows = min(PERM_ROWS, n)
    nblk = n // rows
    return pl.pallas_call(
        functools.partial(_dispatch_kernel, block_rows=EXPERT_ROWS),
        grid_spec=pltpu.PrefetchScalarGridSpec(
            num_scalar_prefetch=2,
            grid=(nblk,),
            in_specs=[_table_spec(0, nblk), _table_spec(0, nblk), _table_spec(0, nblk),
                      pl.BlockSpec((rows, d), lambda i, *_: (i, 0)),
                      pl.BlockSpec((rows, TOP_K), lambda i, *_: (i, 0))],
            out_specs=pl.BlockSpec(memory_space=pl.ANY),
            scratch_shapes=[pltpu.VMEM((2, PERM_LOCAL, d // 2), U32), pltpu.VMEM((EXPERT_ROWS, d // 2), U32),
                            pltpu.VMEM((N_EXPERTS, PERM_GRAN, d // 2), U32), pltpu.SMEM((1,), I32),
                            pltpu.SemaphoreType.DMA((2,)), pltpu.SemaphoreType.DMA(())],
        ),
        out_shape=jax.ShapeDtypeStruct((total_rows, d // 2), U32),
        compiler_params=pltpu.CompilerParams(dimension_semantics=("arbitrary",), vmem_limit_bytes=VMEM_LIMIT_BYTES),
        name=f"dispatch_l{layer}",
    )(pend, pcnt, cnt_tbl, off_tbl, dst_tbl, h, pos)


def _combine_kernel(cnt_ref, off_ref, dst_ref, ncnt_ref, noff_ref, ndst_ref, h_ref, gate_ref, pos_ref,
                    lg_ref, lb_ref, ys_ref, o_ref, yc_ref, tot_ref, sem, *, alpha):
    i = pl.program_id(0)
    slot = i % 2

    def copy(s, local_row, global_row, size):
        return pltpu.make_async_copy(ys_ref.at[pl.ds(global_row, size)],
                                     yc_ref.at[s, pl.ds(local_row, size)], sem.at[s])

    @pl.when(i == 0)
    def _():
        yc_ref[...] = jnp.zeros_like(yc_ref)
        tot_ref[0] = _for_each_run(cnt_ref, off_ref, dst_ref, 0, lambda lr, gr: copy(0, lr, gr, PERM_CHUNK).start())

    @pl.when(i + 1 < pl.num_programs(0))
    def _():
        tot_ref[1 - slot] = _for_each_run(ncnt_ref, noff_ref, ndst_ref, (i + 1) % TABLE_GROUP,
                                          lambda lr, gr: copy(1 - slot, lr, gr, PERM_CHUNK).start())

    @pl.when(tot_ref[slot] > 0)
    def _():
        copy(slot, 0, 0, tot_ref[slot] * PERM_CHUNK).wait()

    pos = pos_ref[...]
    gates = gate_ref[...]
    lane = lax.broadcasted_iota(I32, (pos.shape[0], PERM_LOCAL), 1)
    q = jnp.where(lane == pos[:, 0:1], gates[:, 0:1], 0.0)
    for k in range(1, TOP_K):
        q = q + jnp.where(lane == pos[:, k:k + 1], gates[:, k:k + 1], 0.0)
    qb = q.astype(BF16)
    y_lo, y_hi = _unpack_bf16_pair(yc_ref[slot])
    ffn = jnp.concatenate([_dot(qb, y_lo), _dot(qb, y_hi)], axis=-1)
    o_ref[...] = _layer_norm(alpha * h_ref[...] + ffn, lg_ref[...], lb_ref[...])


def _combine(h, gates, pos, cnt_tbl, off_tbl, dst_tbl, ys, ln_g, ln_b, layer, alpha):
    n, d = h.shape
    rows = min(PERM_ROWS, n)
    nblk = n // rows
    return pl.pallas_call(
        functools.partial(_combine_kernel, alpha=alpha),
        grid=(nblk,),
        in_specs=[_table_spec(0, nblk), _table_spec(0, nblk), _table_spec(0, nblk),
                  _table_spec(1, nblk), _table_spec(1, nblk), _table_spec(1, nblk),
                  pl.BlockSpec((rows, d), lambda i: (i, 0)),
                  pl.BlockSpec((rows, TOP_K), lambda i: (i, 0)),
                  pl.BlockSpec((rows, TOP_K), lambda i: (i, 0)),
                  pl.BlockSpec((None, 1, d), lambda i: (layer, 0, 0)),
                  pl.BlockSpec((None, 1, d), lambda i: (layer, 0, 0)),
                  pl.BlockSpec(memory_space=pl.ANY)],
        out_specs=pl.BlockSpec((rows, d), lambda i: (i, 0)),
        out_shape=jax.ShapeDtypeStruct((n, d), F32),
        scratch_shapes=[pltpu.VMEM((2, PERM_LOCAL, d // 2), U32), pltpu.SMEM((2,), I32),
                        pltpu.SemaphoreType.DMA((2,))],
        compiler_params=pltpu.CompilerParams(dimension_semantics=("arbitrary",), vmem_limit_bytes=VMEM_LIMIT_BYTES),
        name=f"combine_l{layer}",
    )(cnt_tbl, off_tbl, dst_tbl, cnt_tbl, off_tbl, dst_tbl, h, gates, pos, ln_g, ln_b, ys)


def _expert_kernel(be_ref, nact_ref, x_ref, wgu_ref, bgu_ref, wd_ref, bd_ref, y_ref, wgu_b_ref, wd_b_ref):
    j = pl.program_id(0)
    d_ff = wd_ref.shape[0]
    half = y_ref.shape[1]

    @pl.when((j == 0) | (be_ref[j] != be_ref[jnp.maximum(j - 1, 0)]))
    def _():
        wgu_b_ref[...] = wgu_ref[...].astype(BF16)
        wd_b_ref[...] = wd_ref[...].astype(BF16)

    @pl.when(j < nact_ref[0])
    def _():
        x_lo, x_hi = _unpack_bf16_pair(x_ref[...])
        gu = _dot(x_lo, wgu_b_ref[0:half, :]) + _dot(x_hi, wgu_b_ref[half:2 * half, :]) + bgu_ref[...]
        gate = jnp.minimum(gu[:, :d_ff], SWIGLU_LIMIT)
        up = jnp.clip(gu[:, d_ff:], -SWIGLU_LIMIT, SWIGLU_LIMIT)
        act = (up + 1.0) * gate * _sigmoid(SWIGLU_ALPHA * gate)
        y = _dot(act.astype(BF16), wd_b_ref[...]) + bd_ref[...]
        y_ref[...] = _pack_bf16_pair(_round_bf16(y[:, :half]), _round_bf16(y[:, half:]))

    @pl.when(j >= nact_ref[0])
    def _():
        y_ref[...] = jnp.zeros_like(y_ref)


def _experts(xs, blk_expert, nact, wgu, bgu, wd, bd, layer):
    total_rows, half = xs.shape
    nblk = total_rows // EXPERT_ROWS
    d = wgu.shape[2]
    d_ff = wd.shape[2]

    def row_block(j, be, na):
        return (j, 0)

    def of_expert(j, be, na):
        return (layer, be[j], 0, 0)

    return pl.pallas_call(
        _expert_kernel,
        grid_spec=pltpu.PrefetchScalarGridSpec(
            num_scalar_prefetch=2,
            grid=(nblk,),
            in_specs=[pl.BlockSpec((EXPERT_ROWS, half), row_block),
                      pl.BlockSpec((None, None, d, 2 * d_ff), of_expert),
                      pl.BlockSpec((None, None, 1, 2 * d_ff), of_expert),
                      pl.BlockSpec((None, None, d_ff, d), of_expert),
                      pl.BlockSpec((None, None, 1, d), of_expert)],
            out_specs=pl.BlockSpec((EXPERT_ROWS, half), row_block),
            scratch_shapes=[pltpu.VMEM((d, 2 * d_ff), BF16), pltpu.VMEM((d_ff, d), BF16)],
        ),
        out_shape=jax.ShapeDtypeStruct((total_rows, half), U32),
        compiler_params=pltpu.CompilerParams(dimension_semantics=("arbitrary",), vmem_limit_bytes=VMEM_LIMIT_BYTES),
        name=f"experts_l{layer}",
    )(blk_expert, nact, xs, wgu, bgu, wd, bd)


def _moe(h, w_router_p, b_router_p, wgu, bgu, wd, bd, ln_g, ln_b, layer, alpha):
    n, d = h.shape
    gates, pos, cnt_tbl, off_tbl, before_tbl, totals = _router(h, w_router_p, b_router_p, layer)
    counts = totals[0, :N_EXPERTS]
    padded = (counts + PERM_CHUNK + EXPERT_ROWS - 1) // EXPERT_ROWS * EXPERT_ROWS
    padded = jnp.where(counts > 0, padded, 0)
    pend = jnp.cumsum(padded).astype(I32)
    pstart = pend - padded
    nblk = (n * TOP_K + N_EXPERTS * PERM_CHUNK) // EXPERT_ROWS + N_EXPERTS
    nact = (pend[-1] // EXPERT_ROWS).reshape(1).astype(I32)
    blk = jnp.minimum(jnp.arange(nblk, dtype=I32), nact[0] - 1) * EXPERT_ROWS
    blk_expert = jnp.minimum(jnp.sum(pend[None, :] <= blk[:, None], axis=1), N_EXPERTS - 1).astype(I32)
    start_row = jnp.zeros((LANES,), I32).at[:N_EXPERTS].set(pstart)

    def flat(tbl):
        tbl = tbl.reshape(-1)
        return jnp.pad(tbl, (0, -tbl.shape[0] % SMEM_TABLE))

    cnt_f, off_f = flat(cnt_tbl), flat(off_tbl)
    dst_f = flat(before_tbl + start_row[None, None, :])

    xs = _dispatch(h, pos, cnt_f, off_f, dst_f, pend, padded.astype(I32), nblk * EXPERT_ROWS, layer)
    ys = _experts(xs, blk_expert, nact, wgu, bgu, wd, bd, layer)
    return _combine(h, gates, pos, cnt_f, off_f, dst_f, ys, ln_g, ln_b, layer, alpha)


def _relayout_w_in(w_in):
    depth, d, _ = w_in.shape
    out = jnp.zeros((depth, d, D_IN_PAD), BF16)
    for g in range(len(IN_SPLITS)):
        lo, hi = _COLS[g]
        out = out.at[:, :, lo:hi].set(w_in[:, :, _SRC_OFF[g]:_SRC_OFF[g] + IN_SPLITS[g]].astype(BF16))
    return out


def kernel(x, ln_in_g, ln_in_b, w_in, w_gla_gate_expand, b_gla_gate, gla_norm_g, hgrn_lb_raw, hgrn_norm_g, conv_w, w_out, ln_mix_g, ln_mix_b, w_router, b_router, w_gate_up, b_gate_up, w_down, b_down, ln_ffn_g, ln_ffn_b):
    bsz, t, d = x.shape
    depth = w_in.shape[0]
    alpha = (2 * depth) ** 0.25
    n = bsz * t

    w_in_r = _relayout_w_in(w_in)
    w_exp = jnp.zeros((depth, LANES, GLA_K_DIM), F32).at[:, :GLA_GATE_RANK, :].set(w_gla_gate_expand)
    b_gate = b_gla_gate.reshape(depth, 1, GLA_K_DIM)
    gla_g = jnp.tile(gla_norm_g, (1, GLA_HEADS)).reshape(depth, 1, GLA_V_DIM)
    hgrn_g = jnp.tile(hgrn_norm_g, (1, HGRN_HEADS)).reshape(depth, 1, HGRN_I_DIM)
    w_out_b = w_out.astype(BF16)
    w_router_p = jnp.zeros((depth, d, LANES), F32).at[:, :, :N_EXPERTS].set(w_router)
    b_router_p = jnp.zeros((depth, 1, LANES), F32).at[:, 0, :N_EXPERTS].set(b_router)
    wgu, wd = w_gate_up, w_down
    bgu = b_gate_up.reshape(depth, N_EXPERTS, 1, -1)
    bd = b_down.reshape(depth, N_EXPERTS, 1, -1)
    row = lambda a: a.reshape(depth, 1, d)
    consts = _mixer_consts(min(CUMSUM_ROWS, t))

    h = x
    for l in range(depth):
        h = _mixer(h, ln_in_g.reshape(1, d), ln_in_b.reshape(1, d), w_in_r, w_exp, b_gate, gla_g, hgrn_lb_raw,
                   hgrn_g, conv_w, w_out_b, row(ln_mix_g), row(ln_mix_b), consts,
                   layer=l, pre_ln=(l == 0), alpha=alpha)
        h = _moe(h.reshape(n, d), w_router_p, b_router_p, wgu, bgu, wd, bd, row(ln_ffn_g), row(ln_ffn_b),
                 l, alpha).reshape(bsz, t, d)
    return h
```

```python
import functools

import jax
import jax.numpy as jnp
from jax import lax
from jax.experimental import pallas as pl
from jax.experimental.pallas import tpu as pltpu

F32 = jnp.float32
BF16 = jnp.bfloat16
U32 = jnp.uint32
I32 = jnp.int32

GLA_HEADS = 4
GLA_HEAD_K = 64
GLA_HEAD_V = 128
GLA_GATE_RANK = 16
GLA_GATE_NORMALIZER = 16.0
HGRN_HEADS = 4
HGRN_HEAD_F = 64
HGRN_HEAD_I = 64
CONV_CHANNELS = 256
CONV_WIDTH = 3
CHUNK = 64
N_EXPERTS = 32
TOP_K = 4
SWIGLU_LIMIT = 7.0
SWIGLU_ALPHA = 1.702
LN_EPS = 1e-5
RMS_EPS = 1e-6

GLA_K_DIM = GLA_HEADS * GLA_HEAD_K
GLA_V_DIM = GLA_HEADS * GLA_HEAD_V
HGRN_F_DIM = HGRN_HEADS * HGRN_HEAD_F
HGRN_I_DIM = HGRN_HEADS * HGRN_HEAD_I
IN_SPLITS = (GLA_K_DIM, GLA_K_DIM, GLA_V_DIM, GLA_GATE_RANK, GLA_V_DIM,
             HGRN_F_DIM, HGRN_F_DIM, HGRN_I_DIM, HGRN_I_DIM,
             CONV_CHANNELS, CONV_CHANNELS, CONV_CHANNELS)

LANES = 128
SUBLANES = 8
VMEM_LIMIT_BYTES = 56 * 1024 * 1024

MIX_ROWS = 512
CUMSUM_ROWS = 256
PERM_ROWS = 256
PERM_GRAN = SUBLANES
PERM_CHUNK_LOG2 = 4
PERM_CHUNK = 1 << PERM_CHUNK_LOG2
PERM_LOCAL = -(-(PERM_ROWS * TOP_K + N_EXPERTS * (PERM_GRAN + PERM_CHUNK - 2)) // LANES) * LANES
EXPERT_ROWS = 512
SMEM_TABLE = 1024
TABLE_GROUP = SMEM_TABLE // LANES

_ORDER = (0, 1, 2, 4, 5, 6, 7, 8, 9, 10, 11, 3)
_SRC_OFF = [sum(IN_SPLITS[:i]) for i in range(len(IN_SPLITS))]
_COLS = {}
_off = 0
for _g in _ORDER:
    _w = IN_SPLITS[_g]
    _COLS[_g] = (_off, _off + _w)
    _off += -(-_w // LANES) * LANES
D_IN_PAD = _off
(C_GQ, C_GK, C_GV, C_GLR, C_GOG, C_HQ, C_HF, C_HI, C_HOG, C_CB, C_CC, C_CH) = (_COLS[i] for i in range(12))


def _layer_norm(x, g, b):
    mu = jnp.mean(x, axis=-1, keepdims=True)
    xc = x - mu
    var = jnp.mean(xc * xc, axis=-1, keepdims=True)
    return xc * lax.rsqrt(var + LN_EPS) * g + b


def _sigmoid(x):
    return 1.0 / (1.0 + jnp.exp(-x))


def _dot(a, b):
    return jnp.dot(a, b, preferred_element_type=F32)


def _dot_nt(a, b):
    return lax.dot_general(a, b, (((1,), (1,)), ((), ())), preferred_element_type=F32)


def _dot_tn(a, b):
    return lax.dot_general(a, b, (((0,), (0,)), ((), ())), preferred_element_type=F32)


def _dot_f32(a, b):
    return lax.dot_general(a, b, (((1,), (0,)), ((), ())), precision=lax.Precision.HIGHEST,
                           preferred_element_type=F32)


def _split_dot(a01, x):
    hi = x.astype(BF16)
    lo = (x - hi.astype(F32)).astype(BF16)
    return _dot(a01, hi) + _dot(a01, lo)


def _rows_to_sublanes(rows):
    sub = lax.broadcasted_iota(I32, (len(rows), rows[0].shape[1]), 0)
    out = jnp.broadcast_to(rows[-1], sub.shape)
    for k in range(len(rows) - 2, -1, -1):
        out = jnp.where(sub == k, rows[k], out)
    return out


def _pack_bf16_pair(lo, hi):
    return lax.bitcast_convert_type(hi, U32) | (lax.bitcast_convert_type(lo, U32) >> 16)


def _unpack_bf16_pair(word):
    lo = lax.bitcast_convert_type(word << 16, F32).astype(BF16)
    hi = lax.bitcast_convert_type(word & jnp.uint32(0xFFFF0000), F32).astype(BF16)
    return lo, hi


def _round_bf16(x):
    return x.astype(BF16).astype(F32)


def _chunk_recurrence(q, k, v, g, st_ref, tril, causal, bmask, heads, dk, dv):
    rows = q.shape[0]
    cum = tril.shape[0]
    gc = jnp.concatenate([_split_dot(tril, g[r:r + cum]) for r in range(0, rows, cum)], axis=0)
    lane_k = lax.broadcasted_iota(I32, (1, heads * dk), 1) // dk
    lane_v = lax.broadcasted_iota(I32, (1, heads * dv), 1) // dv
    st = st_ref[...]
    outs = []
    for c in range(rows // CHUNK):
        sl = slice(c * CHUNK, (c + 1) * CHUNK)
        gcc, qc, kc, vc = gc[sl], q[sl], k[sl], v[sl]
        mid = gcc[CHUNK // 2 - 1:CHUNK // 2]
        last = gcc[CHUNK - 1:CHUNK]
        qt = qc * jnp.exp(gcc - mid)
        kt = kc * jnp.exp(mid - gcc)
        qi = qc * jnp.exp(gcc)
        ks = kc * jnp.exp(last - gcc)
        vb = vc.astype(BF16)
        qstack = jnp.concatenate([jnp.where(lane_k == h, qt, 0.0) for h in range(heads)], axis=0)
        a = _dot_nt(qstack.astype(BF16), kt.astype(BF16))
        a = jnp.where(causal, a, 0.0)
        o_all = _dot(a.astype(BF16), vb)
        o_intra = jnp.where(lane_v == 0, o_all[0:CHUNK], 0.0)
        for h in range(1, heads):
            o_intra = o_intra + jnp.where(lane_v == h, o_all[h * CHUNK:(h + 1) * CHUNK], 0.0)
        o_inter = _dot_nt(qi.astype(BF16), st.astype(BF16))
        u = _dot_tn(vb, ks.astype(BF16))
        st = jnp.exp(last) * st + u * bmask
        outs.append(o_intra + o_inter)
    st_ref[...] = st
    return jnp.concatenate(outs, axis=0)


def _gated_head_rmsnorm(o, gate, g_tiled, heads, dv):
    lane_v = lax.broadcasted_iota(I32, (1, heads * dv), 1) // dv
    sq = o * o
    ms = jnp.zeros_like(o)
    for h in range(heads):
        m = lane_v == h
        ms = jnp.where(m, jnp.sum(jnp.where(m, sq, 0.0), axis=-1, keepdims=True), ms)
    ms = ms * (1.0 / dv)
    return o * lax.rsqrt(ms + RMS_EPS) * g_tiled * (gate * _sigmoid(gate))


def _mixer_kernel(x_ref, lng_ref, lnb_ref, win_ref, wexp_ref, bgate_ref, glag_ref, lbraw_ref, hgg_ref,
                  convw_ref, wout_ref, mg_ref, mb_ref, tril_ref, causal_ref, bmg_ref, bmh_ref,
                  o_ref, sg_ref, sh_ref, cc_ref, *, layer, pre_ln, alpha):
    @pl.when(pl.program_id(1) == 0)
    def _():
        sg_ref[...] = jnp.zeros_like(sg_ref)
        sh_ref[...] = jnp.zeros_like(sh_ref)
        cc_ref[...] = jnp.zeros_like(cc_ref)

    x = x_ref[...]
    h = _layer_norm(x, lng_ref[...], lnb_ref[...]) if pre_ln else x
    hb = h.astype(BF16)
    rows = x.shape[0]

    def proj(cols):
        return _dot(hb, win_ref[:, cols[0]:cols[0] + -(-(cols[1] - cols[0]) // LANES) * LANES])

    tril = tril_ref[...]
    causal = causal_ref[...] > 0.5

    q = proj(C_GQ) * (GLA_HEAD_K ** -0.5)
    gate_logits = _dot_f32(proj(C_GLR), wexp_ref[...]) + bgate_ref[...]
    log_a = (jnp.minimum(gate_logits, 0.0) - jnp.log(1.0 + jnp.exp(-jnp.abs(gate_logits)))) * (1.0 / GLA_GATE_NORMALIZER)
    o_gla = _chunk_recurrence(q, proj(C_GK), proj(C_GV), log_a, sg_ref, tril, causal, bmg_ref[...],
                              GLA_HEADS, GLA_HEAD_K, GLA_HEAD_V)
    y_gla = _gated_head_rmsnorm(o_gla, proj(C_GOG), glag_ref[...], GLA_HEADS, GLA_HEAD_V)

    raw = lbraw_ref[...]
    e = jnp.exp(raw - jnp.max(raw, axis=0, keepdims=True))
    lb = jnp.zeros_like(e[0:1])
    for i in range(1, layer + 1):
        lb = lb + e[i:i + 1]
    lb = lb / jnp.sum(e, axis=0, keepdims=True)
    f = lb + (1.0 - lb) * _sigmoid(proj(C_HF))
    hq = proj(C_HQ)
    o_h = _chunk_recurrence(hq * _sigmoid(hq), 1.0 - f, proj(C_HI), jnp.log(f), sh_ref, tril, causal,
                            bmh_ref[...], HGRN_HEADS, HGRN_HEAD_F, HGRN_HEAD_I)
    y_h = _gated_head_rmsnorm(o_h, proj(C_HOG), hgg_ref[...], HGRN_HEADS, HGRN_HEAD_I)

    u = proj(C_CC) * proj(C_CH)
    prev = cc_ref[...]
    row = lax.broadcasted_iota(I32, (rows, 1), 0)
    u1 = jnp.where(row == 0, prev[SUBLANES - 1:SUBLANES], pltpu.roll(u, 1, 0))
    u2 = jnp.where(row == 0, prev[SUBLANES - 2:SUBLANES - 1],
                   jnp.where(row == 1, prev[SUBLANES - 1:SUBLANES], pltpu.roll(u, 2, 0)))
    cc_ref[...] = u[rows - SUBLANES:rows]
    cw = convw_ref[...]
    y_conv = proj(C_CB) * (cw[0:1] * u2 + cw[1:2] * u1 + cw[2:3] * u)

    o1, o2 = GLA_V_DIM, GLA_V_DIM + HGRN_I_DIM
    mix = (_dot(y_gla.astype(BF16), wout_ref[0:o1, :]) + _dot(y_h.astype(BF16), wout_ref[o1:o2, :])
           + _dot(y_conv.astype(BF16), wout_ref[o2:o2 + CONV_CHANNELS, :]))
    o_ref[...] = _layer_norm(alpha * h + mix, mg_ref[...], mb_ref[...])


def _mixer(x, ln_g, ln_b, w_in_r, w_exp, b_gate, gla_g, lb_raw, hgrn_g, conv_w, w_out, mg, mb,
           consts, *, layer, pre_ln, alpha):
    bsz, t, d = x.shape
    rows = min(MIX_ROWS, t)
    tril, causal, bmg, bmh = consts
    assert t % rows == 0 and rows % tril.shape[0] == 0 and tril.shape[0] % CHUNK == 0

    def lay(shape):
        return pl.BlockSpec((None,) + shape, lambda b, i: (layer,) + (0,) * len(shape))

    def whole(a):
        return pl.BlockSpec(a.shape, lambda b, i: (0,) * a.ndim)

    return pl.pallas_call(
        functools.partial(_mixer_kernel, layer=layer, pre_ln=pre_ln, alpha=alpha),
        grid=(bsz, t // rows),
        in_specs=[
            pl.BlockSpec((None, rows, d), lambda b, i: (b, i, 0)),
            whole(ln_g), whole(ln_b),
            lay((d, D_IN_PAD)), lay((LANES, GLA_K_DIM)), lay((1, GLA_K_DIM)), lay((1, GLA_V_DIM)),
            whole(lb_raw), lay((1, HGRN_I_DIM)), lay((CONV_WIDTH, CONV_CHANNELS)), lay((d, d)),
            lay((1, d)), lay((1, d)),
            whole(tril), whole(causal), whole(bmg), whole(bmh),
        ],
        out_specs=pl.BlockSpec((None, rows, d), lambda b, i: (b, i, 0)),
        out_shape=jax.ShapeDtypeStruct((bsz, t, d), F32),
        scratch_shapes=[pltpu.VMEM((GLA_V_DIM, GLA_K_DIM), F32), pltpu.VMEM((HGRN_I_DIM, HGRN_F_DIM), F32),
                        pltpu.VMEM((SUBLANES, CONV_CHANNELS), F32)],
        compiler_params=pltpu.CompilerParams(dimension_semantics=("arbitrary", "arbitrary"),
                                             vmem_limit_bytes=VMEM_LIMIT_BYTES),
        name=f"mixer_l{layer}",
    )(x, ln_g, ln_b, w_in_r, w_exp, b_gate, gla_g, lb_raw, hgrn_g, conv_w, w_out, mg, mb, tril, causal, bmg, bmh)


def _mixer_consts(rows):
    r = jnp.arange(rows)
    tril = ((r[:, None] // CHUNK == r[None, :] // CHUNK) & (r[None, :] <= r[:, None])).astype(BF16)
    i = jnp.arange(GLA_HEADS * CHUNK) % CHUNK
    causal = (jnp.arange(CHUNK)[None, :] <= i[:, None]).astype(F32)
    bmg = (jnp.arange(GLA_V_DIM)[:, None] // GLA_HEAD_V == jnp.arange(GLA_K_DIM)[None, :] // GLA_HEAD_K).astype(F32)
    bmh = (jnp.arange(HGRN_I_DIM)[:, None] // HGRN_HEAD_I == jnp.arange(HGRN_F_DIM)[None, :] // HGRN_HEAD_F).astype(F32)
    return tril, causal, bmg, bmh


def _router_kernel(h_ref, wr_ref, br_ref, ut_ref, lt_ref, gate_ref, pos_ref, cnt_ref, off_ref, before_ref,
                   total_ref, carry_ref):
    @pl.when(pl.program_id(0) == 0)
    def _():
        carry_ref[...] = jnp.zeros_like(carry_ref)

    h = h_ref[...]
    h_hi = h.astype(BF16)
    h_lo = (h - h_hi.astype(F32)).astype(BF16)
    w = wr_ref[...]
    by_hi = _dot_nt(w, h_hi)
    l = by_hi[0:N_EXPERTS] + by_hi[N_EXPERTS:2 * N_EXPERTS] + _dot_nt(w[0:N_EXPERTS], h_lo) + br_ref[...]
    expert = lax.broadcasted_iota(I32, l.shape, 0).astype(F32)
    neg = jnp.float32(-jnp.inf)
    tops, idxs = [], []
    for _ in range(TOP_K):
        m = jnp.max(l, axis=0, keepdims=True)
        ik = jnp.min(jnp.where(l == m, expert, float(N_EXPERTS)), axis=0, keepdims=True)
        tops.append(m)
        idxs.append(ik)
        l = jnp.where(expert == ik, neg, l)
    es = [jnp.exp(t - tops[0]) for t in tops]
    denom = es[0] + es[1] + es[2] + es[3]
    onehot = jnp.zeros_like(l)
    for ik in idxs:
        onehot = onehot + jnp.where(expert == ik, 1.0, 0.0)

    carry = carry_ref[...]
    earlier = _dot(onehot.astype(BF16), ut_ref[...])
    cnt = jnp.broadcast_to(jnp.sum(onehot, axis=1, keepdims=True), carry.shape)
    phase = carry - PERM_GRAN * jnp.floor(carry * (1.0 / PERM_GRAN))
    run = jnp.where(cnt > 0.0, jnp.ceil((phase + cnt) * (1.0 / PERM_CHUNK)) * PERM_CHUNK, 0.0)
    off = _dot_f32(lt_ref[...], run)
    local = earlier + (off + phase)[:, 0:1]
    pos = [jnp.sum(jnp.where(expert == ik, local, 0.0), axis=0, keepdims=True) for ik in idxs]

    cnt_ref[...] = cnt.astype(I32)
    off_ref[...] = off.astype(I32)
    before_ref[...] = carry.astype(I32)
    carry = carry + cnt
    carry_ref[...] = carry
    total_ref[...] = carry.astype(I32)
    gate_ref[...] = _rows_to_sublanes([e / denom for e in es])
    pos_ref[...] = _rows_to_sublanes(pos).astype(I32)


def _router(h, w_router_p, b_router_p, layer):
    n, d = h.shape
    rows = min(PERM_ROWS, n)
    assert n % rows == 0
    nblk = n // rows
    r = jnp.arange(rows)
    upper = (r[:, None] < r[None, :]).astype(BF16)
    e = jnp.arange(N_EXPERTS)
    lower = (e[None, :] < e[:, None]).astype(F32)
    pair = pl.BlockSpec((TOP_K, rows), lambda i: (0, i))
    table = pl.BlockSpec((None, N_EXPERTS, LANES), lambda i: (i, 0, 0))
    table_shape = jax.ShapeDtypeStruct((nblk, N_EXPERTS, LANES), I32)
    return pl.pallas_call(
        _router_kernel,
        grid=(nblk,),
        in_specs=[pl.BlockSpec((rows, d), lambda i: (i, 0)),
                  pl.BlockSpec((None, 2 * N_EXPERTS, d), lambda i: (layer, 0, 0)),
                  pl.BlockSpec((None, N_EXPERTS, 1), lambda i: (layer, 0, 0)),
                  pl.BlockSpec((rows, rows), lambda i: (0, 0)),
                  pl.BlockSpec((N_EXPERTS, N_EXPERTS), lambda i: (0, 0))],
        out_specs=[pair, pair, table, table, table, pl.BlockSpec((N_EXPERTS, LANES), lambda i: (0, 0))],
        out_shape=[jax.ShapeDtypeStruct((TOP_K, n), F32), jax.ShapeDtypeStruct((TOP_K, n), I32),
                   table_shape, table_shape, table_shape, jax.ShapeDtypeStruct((N_EXPERTS, LANES), I32)],
        scratch_shapes=[pltpu.VMEM((N_EXPERTS, LANES), F32)],
        compiler_params=pltpu.CompilerParams(dimension_semantics=("arbitrary",), vmem_limit_bytes=VMEM_LIMIT_BYTES),
        name=f"router_l{layer}",
    )(h, w_router_p, b_router_p, upper, lower)


def _for_each_chunk(cnt_ref, off_ref, dst_ref, table_row, chunk_fn):
    base = table_row * LANES
    chunks = jnp.int32(0)
    for e in range(N_EXPERTS):
        n = cnt_ref[base + e]
        loc = off_ref[base + e]
        dst = dst_ref[base + e]
        phase = dst & (PERM_GRAN - 1)
        dst0 = dst - phase
        n_chunks = jnp.where(n > 0, (phase + n + (PERM_CHUNK - 1)) >> PERM_CHUNK_LOG2, 0)

        def body(c, carry, loc=loc, dst0=dst0):
            chunk_fn(pl.multiple_of(loc + c * PERM_CHUNK, PERM_CHUNK), pl.multiple_of(dst0 + c * PERM_CHUNK, PERM_GRAN))
            return carry

        lax.fori_loop(0, n_chunks, body, 0)
        chunks = chunks + n_chunks
    return chunks


def _dispatch_kernel(pend_ref, pcnt_ref, cnt_ref, off_ref, dst_ref, h_ref, pos_ref, xs_ref,
                     xc_ref, zbuf_ref, tail_ref, tot_ref, sem, zsem, *, block_rows):
    i = pl.program_id(0)
    slot = i % 2

    def fill(first_row):
        start = pl.multiple_of(first_row, block_rows)
        return pltpu.make_async_copy(zbuf_ref, xs_ref.at[pl.ds(start, block_rows)], zsem)

    @pl.when(i == 0)
    def _():
        zbuf_ref[...] = jnp.zeros_like(zbuf_ref)
        tail_ref[...] = jnp.zeros_like(tail_ref)
        used_blocks = pend_ref[N_EXPERTS - 1] // block_rows
        all_blocks = xs_ref.shape[0] // block_rows
        for e in range(N_EXPERTS):
            @pl.when(pcnt_ref[e] > 0)
            def _():
                fill(pend_ref[e] - block_rows).start()
        lax.fori_loop(used_blocks, all_blocks, lambda j, c: (fill(j * block_rows).start(), c)[1], 0)
        for e in range(N_EXPERTS):
            @pl.when(pcnt_ref[e] > 0)
            def _():
                fill(pend_ref[e] - block_rows).wait()
        lax.fori_loop(used_blocks, all_blocks, lambda j, c: (fill(j * block_rows).wait(), c)[1], 0)

    pos = pos_ref[...]
    local_row = lax.broadcasted_iota(I32, (PERM_LOCAL, pos.shape[1]), 0)
    p = jnp.where(local_row == pos[0:1], 1.0, 0.0)
    for k in range(1, TOP_K):
        p = p + jnp.where(local_row == pos[k:k + 1], 1.0, 0.0)
    pb = p.astype(BF16)
    hb = h_ref[...].astype(BF16)
    half = hb.shape[1] // 2
    xc_ref[slot] = _pack_bf16_pair(_dot(pb, hb[:, :half]), _dot(pb, hb[:, half:]))

    def copy(s, local_row, global_row, size):
        return pltpu.make_async_copy(xc_ref.at[s, pl.ds(local_row, size)],
                                     xs_ref.at[pl.ds(global_row, size)], sem.at[s])

    def drain(s):
        @pl.when(tot_ref[0] > 0)
        def _():
            copy(s, 0, 0, tot_ref[0] * PERM_CHUNK).wait()

    @pl.when(i > 0)
    def _():
        drain(1 - slot)

    row8 = lax.broadcasted_iota(I32, (PERM_GRAN, 1), 0)

    def carry_partial_group(e, n, loc, phase):
        first = pl.ds(pl.multiple_of(loc, PERM_GRAN), PERM_GRAN)
        merged = jnp.where((row8 < phase) & (n > 0), tail_ref[e], xc_ref[slot, first, :])
        xc_ref[slot, first, :] = merged
        end = pl.ds(pl.multiple_of(loc + ((phase + n) & -PERM_GRAN), PERM_GRAN), PERM_GRAN)
        tail_ref[e] = jnp.where(n > 0, xc_ref[slot, end, :], tail_ref[e])

    base = (i % TABLE_GROUP) * LANES
    for e in range(N_EXPERTS):
        carry_partial_group(e, cnt_ref[base + e], off_ref[base + e], dst_ref[base + e] & (PERM_GRAN - 1))
    tot_ref[0] = _for_each_chunk(cnt_ref, off_ref, dst_ref, i % TABLE_GROUP,
                                 lambda lr, gr: copy(slot, lr, gr, PERM_CHUNK).start())

    @pl.when(i == pl.num_programs(0) - 1)
    def _():
        drain(slot)


def _table_spec(shift, nblk):
    return pl.BlockSpec((SMEM_TABLE,), lambda i, *_: (jnp.minimum(i + shift, nblk - 1) // TABLE_GROUP,),
                        memory_space=pltpu.SMEM)


def _dispatch(h, pos, cnt_tbl, off_tbl, dst_tbl, pend, pcnt, total_rows, layer):
    n, d = h.shape
    rows = min(PERM_ROWS, n)
    nblk = n // rows
    return pl.pallas_call(
        functools.partial(_dispatch_kernel, block_rows=EXPERT_ROWS),
        grid_spec=pltpu.PrefetchScalarGridSpec(
            num_scalar_prefetch=2,
            grid=(nblk,),
            in_specs=[_table_spec(0, nblk), _table_spec(0, nblk), _table_spec(0, nblk),
                      pl.BlockSpec((rows, d), lambda i, *_: (i, 0)),
                      pl.BlockSpec((TOP_K, rows), lambda i, *_: (0, i))],
            out_specs=pl.BlockSpec(memory_space=pl.ANY),
            scratch_shapes=[pltpu.VMEM((2, PERM_LOCAL, d // 2), U32), pltpu.VMEM((EXPERT_ROWS, d // 2), U32),
                            pltpu.VMEM((N_EXPERTS, PERM_GRAN, d // 2), U32), pltpu.SMEM((1,), I32),
                            pltpu.SemaphoreType.DMA((2,)), pltpu.SemaphoreType.DMA(())],
        ),
        out_shape=jax.ShapeDtypeStruct((total_rows, d // 2), U32),
        compiler_params=pltpu.CompilerParams(dimension_semantics=("arbitrary",), vmem_limit_bytes=VMEM_LIMIT_BYTES),
        name=f"dispatch_l{layer}",
    )(pend, pcnt, cnt_tbl, off_tbl, dst_tbl, h, pos)


def _combine_kernel(cnt_ref, off_ref, dst_ref, ncnt_ref, noff_ref, ndst_ref, h_ref, gate_ref, pos_ref,
                    lg_ref, lb_ref, ys_ref, o_ref, yc_ref, tot_ref, sem, *, alpha):
    i = pl.program_id(0)
    slot = i % 2

    def copy(s, local_row, global_row, size):
        return pltpu.make_async_copy(ys_ref.at[pl.ds(global_row, size)],
                                     yc_ref.at[s, pl.ds(local_row, size)], sem.at[s])

    @pl.when(i == 0)
    def _():
        yc_ref[...] = jnp.zeros_like(yc_ref)
        tot_ref[0] = _for_each_chunk(cnt_ref, off_ref, dst_ref, 0, lambda lr, gr: copy(0, lr, gr, PERM_CHUNK).start())

    @pl.when(i + 1 < pl.num_programs(0))
    def _():
        tot_ref[1 - slot] = _for_each_chunk(ncnt_ref, noff_ref, ndst_ref, (i + 1) % TABLE_GROUP,
                                            lambda lr, gr: copy(1 - slot, lr, gr, PERM_CHUNK).start())

    @pl.when(tot_ref[slot] > 0)
    def _():
        copy(slot, 0, 0, tot_ref[slot] * PERM_CHUNK).wait()

    pos = pos_ref[...]
    gates = gate_ref[...]
    local_row = lax.broadcasted_iota(I32, (PERM_LOCAL, pos.shape[1]), 0)
    q = jnp.where(local_row == pos[0:1], gates[0:1], 0.0)
    for k in range(1, TOP_K):
        q = q + jnp.where(local_row == pos[k:k + 1], gates[k:k + 1], 0.0)
    qb = q.astype(BF16)
    y_lo, y_hi = _unpack_bf16_pair(yc_ref[slot])
    ffn = jnp.concatenate([_dot_tn(qb, y_lo), _dot_tn(qb, y_hi)], axis=-1)
    o_ref[...] = _layer_norm(alpha * h_ref[...] + ffn, lg_ref[...], lb_ref[...])


def _combine(h, gates, pos, cnt_tbl, off_tbl, dst_tbl, ys, ln_g, ln_b, layer, alpha):
    n, d = h.shape
    rows = min(PERM_ROWS, n)
    nblk = n // rows
    return pl.pallas_call(
        functools.partial(_combine_kernel, alpha=alpha),
        grid=(nblk,),
        in_specs=[_table_spec(0, nblk), _table_spec(0, nblk), _table_spec(0, nblk),
                  _table_spec(1, nblk), _table_spec(1, nblk), _table_spec(1, nblk),
                  pl.BlockSpec((rows, d), lambda i: (i, 0)),
                  pl.BlockSpec((TOP_K, rows), lambda i: (0, i)),
                  pl.BlockSpec((TOP_K, rows), lambda i: (0, i)),
                  pl.BlockSpec((None, 1, d), lambda i: (layer, 0, 0)),
                  pl.BlockSpec((None, 1, d), lambda i: (layer, 0, 0)),
                  pl.BlockSpec(memory_space=pl.ANY)],
        out_specs=pl.BlockSpec((rows, d), lambda i: (i, 0)),
        out_shape=jax.ShapeDtypeStruct((n, d), F32),
        scratch_shapes=[pltpu.VMEM((2, PERM_LOCAL, d // 2), U32), pltpu.SMEM((2,), I32),
                        pltpu.SemaphoreType.DMA((2,))],
        compiler_params=pltpu.CompilerParams(dimension_semantics=("arbitrary",), vmem_limit_bytes=VMEM_LIMIT_BYTES),
        name=f"combine_l{layer}",
    )(cnt_tbl, off_tbl, dst_tbl, cnt_tbl, off_tbl, dst_tbl, h, gates, pos, ln_g, ln_b, ys)


def _expert_kernel(be_ref, nact_ref, x_ref, wgu_ref, bgu_ref, wd_ref, bd_ref, y_ref, wgu_b_ref, wd_b_ref):
    j = pl.program_id(0)
    d_ff = wd_ref.shape[0]
    half = y_ref.shape[1]

    @pl.when((j == 0) | (be_ref[j] != be_ref[jnp.maximum(j - 1, 0)]))
    def _():
        wgu_b_ref[...] = wgu_ref[...].astype(BF16)
        wd_b_ref[...] = wd_ref[...].astype(BF16)

    @pl.when(j < nact_ref[0])
    def _():
        x_lo, x_hi = _unpack_bf16_pair(x_ref[...])
        gu = _dot(x_lo, wgu_b_ref[0:half, :]) + _dot(x_hi, wgu_b_ref[half:2 * half, :]) + bgu_ref[...]
        gate = jnp.minimum(gu[:, :d_ff], SWIGLU_LIMIT)
        up = jnp.clip(gu[:, d_ff:], -SWIGLU_LIMIT, SWIGLU_LIMIT)
        act = (up + 1.0) * gate * _sigmoid(SWIGLU_ALPHA * gate)
        y = _dot(act.astype(BF16), wd_b_ref[...]) + bd_ref[...]
        y_ref[...] = _pack_bf16_pair(_round_bf16(y[:, :half]), _round_bf16(y[:, half:]))

    @pl.when(j >= nact_ref[0])
    def _():
        y_ref[...] = jnp.zeros_like(y_ref)


def _experts(xs, blk_expert, nact, wgu, bgu, wd, bd, layer):
    total_rows, half = xs.shape
    nblk = total_rows // EXPERT_ROWS
    d = wgu.shape[2]
    d_ff = wd.shape[2]

    def row_block(j, be, na):
        return (j, 0)

    def of_expert(j, be, na):
        return (layer, be[j], 0, 0)

    return pl.pallas_call(
        _expert_kernel,
        grid_spec=pltpu.PrefetchScalarGridSpec(
            num_scalar_prefetch=2,
            grid=(nblk,),
            in_specs=[pl.BlockSpec((EXPERT_ROWS, half), row_block),
                      pl.BlockSpec((None, None, d, 2 * d_ff), of_expert),
                      pl.BlockSpec((None, None, 1, 2 * d_ff), of_expert),
                      pl.BlockSpec((None, None, d_ff, d), of_expert),
                      pl.BlockSpec((None, None, 1, d), of_expert)],
            out_specs=pl.BlockSpec((EXPERT_ROWS, half), row_block),
            scratch_shapes=[pltpu.VMEM((d, 2 * d_ff), BF16), pltpu.VMEM((d_ff, d), BF16)],
        ),
        out_shape=jax.ShapeDtypeStruct((total_rows, half), U32),
        compiler_params=pltpu.CompilerParams(dimension_semantics=("arbitrary",), vmem_limit_bytes=VMEM_LIMIT_BYTES),
        name=f"experts_l{layer}",
    )(blk_expert, nact, xs, wgu, bgu, wd, bd)


def _moe(h, w_router_p, b_router_p, wgu, bgu, wd, bd, ln_g, ln_b, layer, alpha):
    n, d = h.shape
    gates, pos, cnt_tbl, off_tbl, before_tbl, totals = _router(h, w_router_p, b_router_p, layer)
    counts = totals[:, 0]
    padded = (counts + PERM_CHUNK + EXPERT_ROWS - 1) // EXPERT_ROWS * EXPERT_ROWS
    padded = jnp.where(counts > 0, padded, 0)
    pend = jnp.cumsum(padded).astype(I32)
    pstart = pend - padded
    nblk = (n * TOP_K + N_EXPERTS * PERM_CHUNK) // EXPERT_ROWS + N_EXPERTS
    nact = (pend[-1] // EXPERT_ROWS).reshape(1).astype(I32)
    blk = jnp.minimum(jnp.arange(nblk, dtype=I32), nact[0] - 1) * EXPERT_ROWS
    blk_expert = jnp.minimum(jnp.sum(pend[None, :] <= blk[:, None], axis=1), N_EXPERTS - 1).astype(I32)

    def flat(tbl):
        tbl = jnp.pad(tbl[:, :, 0], ((0, 0), (0, LANES - N_EXPERTS))).reshape(-1)
        return jnp.pad(tbl, (0, -tbl.shape[0] % SMEM_TABLE))

    cnt_f, off_f = flat(cnt_tbl), flat(off_tbl)
    dst_f = flat(before_tbl + pstart[None, :, None])

    xs = _dispatch(h, pos, cnt_f, off_f, dst_f, pend, padded.astype(I32), nblk * EXPERT_ROWS, layer)
    ys = _experts(xs, blk_expert, nact, wgu, bgu, wd, bd, layer)
    return _combine(h, gates, pos, cnt_f, off_f, dst_f, ys, ln_g, ln_b, layer, alpha)


def _relayout_w_in(w_in):
    depth, d, _ = w_in.shape
    out = jnp.zeros((depth, d, D_IN_PAD), BF16)
    for g in range(len(IN_SPLITS)):
        lo, hi = _COLS[g]
        out = out.at[:, :, lo:hi].set(w_in[:, :, _SRC_OFF[g]:_SRC_OFF[g] + IN_SPLITS[g]].astype(BF16))
    return out


def kernel(x, ln_in_g, ln_in_b, w_in, w_gla_gate_expand, b_gla_gate, gla_norm_g, hgrn_lb_raw, hgrn_norm_g, conv_w, w_out, ln_mix_g, ln_mix_b, w_router, b_router, w_gate_up, b_gate_up, w_down, b_down, ln_ffn_g, ln_ffn_b):
    bsz, t, d = x.shape
    depth = w_in.shape[0]
    alpha = (2 * depth) ** 0.25
    n = bsz * t

    w_in_r = _relayout_w_in(w_in)
    w_exp = jnp.zeros((depth, LANES, GLA_K_DIM), F32).at[:, :GLA_GATE_RANK, :].set(w_gla_gate_expand)
    b_gate = b_gla_gate.reshape(depth, 1, GLA_K_DIM)
    gla_g = jnp.tile(gla_norm_g, (1, GLA_HEADS)).reshape(depth, 1, GLA_V_DIM)
    hgrn_g = jnp.tile(hgrn_norm_g, (1, HGRN_HEADS)).reshape(depth, 1, HGRN_I_DIM)
    w_out_b = w_out.astype(BF16)
    w_router_t = jnp.swapaxes(w_router, 1, 2)
    w_router_hi = w_router_t.astype(BF16)
    w_router_lo = (w_router_t - w_router_hi.astype(F32)).astype(BF16)
    w_router_p = jnp.concatenate([w_router_hi, w_router_lo], axis=1)
    b_router_p = b_router.reshape(depth, N_EXPERTS, 1)
    wgu, wd = w_gate_up, w_down
    bgu = b_gate_up.reshape(depth, N_EXPERTS, 1, -1)
    bd = b_down.reshape(depth, N_EXPERTS, 1, -1)
    row = lambda a: a.reshape(depth, 1, d)
    consts = _mixer_consts(min(CUMSUM_ROWS, t))

    h = x
    for l in range(depth):
        h = _mixer(h, ln_in_g.reshape(1, d), ln_in_b.reshape(1, d), w_in_r, w_exp, b_gate, gla_g, hgrn_lb_raw,
                   hgrn_g, conv_w, w_out_b, row(ln_mix_g), row(ln_mix_b), consts,
                   layer=l, pre_ln=(l == 0), alpha=alpha)
        h = _moe(h.reshape(n, d), w_router_p, b_router_p, wgu, bgu, wd, bd, row(ln_ffn_g), row(ln_ffn_b),
                 l, alpha).reshape(bsz, t, d)
    return h
```

```python
import functools

import jax
import jax.numpy as jnp
from jax import lax
from jax.experimental import pallas as pl
from jax.experimental.pallas import tpu as pltpu

F32 = jnp.float32
BF16 = jnp.bfloat16
U32 = jnp.uint32
I32 = jnp.int32

GLA_HEADS = 4
GLA_HEAD_K = 64
GLA_HEAD_V = 128
GLA_GATE_RANK = 16
GLA_GATE_NORMALIZER = 16.0
HGRN_HEADS = 4
HGRN_HEAD_F = 64
HGRN_HEAD_I = 64
CONV_CHANNELS = 256
CONV_WIDTH = 3
CHUNK = 64
N_EXPERTS = 32
TOP_K = 4
SWIGLU_LIMIT = 7.0
SWIGLU_ALPHA = 1.702
LN_EPS = 1e-5
RMS_EPS = 1e-6

GLA_K_DIM = GLA_HEADS * GLA_HEAD_K
GLA_V_DIM = GLA_HEADS * GLA_HEAD_V
HGRN_F_DIM = HGRN_HEADS * HGRN_HEAD_F
HGRN_I_DIM = HGRN_HEADS * HGRN_HEAD_I
IN_SPLITS = (GLA_K_DIM, GLA_K_DIM, GLA_V_DIM, GLA_GATE_RANK, GLA_V_DIM,
             HGRN_F_DIM, HGRN_F_DIM, HGRN_I_DIM, HGRN_I_DIM,
             CONV_CHANNELS, CONV_CHANNELS, CONV_CHANNELS)

LANES = 128
SUBLANES = 8
VMEM_LIMIT_BYTES = 56 * 1024 * 1024

MIX_ROWS = 512
CUMSUM_ROWS = 256
PERM_ROWS = 256
PERM_GRAN = SUBLANES
PERM_CHUNK_LOG2 = 4
PERM_CHUNK = 1 << PERM_CHUNK_LOG2
PERM_LOCAL = -(-(PERM_ROWS * TOP_K + N_EXPERTS * (PERM_GRAN + PERM_CHUNK - 2)) // LANES) * LANES
EXPERT_ROWS = 512
SMEM_TABLE = 1024
TABLE_GROUP = SMEM_TABLE // LANES

_ORDER = (0, 1, 2, 4, 5, 6, 7, 8, 9, 10, 11, 3)
_SRC_OFF = [sum(IN_SPLITS[:i]) for i in range(len(IN_SPLITS))]
_COLS = {}
_off = 0
for _g in _ORDER:
    _w = IN_SPLITS[_g]
    _COLS[_g] = (_off, _off + _w)
    _off += -(-_w // LANES) * LANES
D_IN_PAD = _off
(C_GQ, C_GK, C_GV, C_GLR, C_GOG, C_HQ, C_HF, C_HI, C_HOG, C_CB, C_CC, C_CH) = (_COLS[i] for i in range(12))


def _layer_norm(x, g, b):
    mu = jnp.mean(x, axis=-1, keepdims=True)
    xc = x - mu
    var = jnp.mean(xc * xc, axis=-1, keepdims=True)
    return xc * lax.rsqrt(var + LN_EPS) * g + b


def _sigmoid(x):
    return 1.0 / (1.0 + jnp.exp(-x))


def _dot(a, b):
    return jnp.dot(a, b, preferred_element_type=F32)


def _dot_nt(a, b):
    return lax.dot_general(a, b, (((1,), (1,)), ((), ())), preferred_element_type=F32)


def _dot_tn(a, b):
    return lax.dot_general(a, b, (((0,), (0,)), ((), ())), preferred_element_type=F32)


def _dot_f32(a, b):
    return lax.dot_general(a, b, (((1,), (0,)), ((), ())), precision=lax.Precision.HIGHEST,
                           preferred_element_type=F32)


def _split_dot(a01, x):
    hi = x.astype(BF16)
    lo = (x - hi.astype(F32)).astype(BF16)
    return _dot(a01, hi) + _dot(a01, lo)


def _rows_to_sublanes(rows):
    sub = lax.broadcasted_iota(I32, (len(rows), rows[0].shape[1]), 0)
    out = jnp.broadcast_to(rows[-1], sub.shape)
    for k in range(len(rows) - 2, -1, -1):
        out = jnp.where(sub == k, rows[k], out)
    return out


def _pack_bf16_pair(lo, hi):
    return lax.bitcast_convert_type(hi, U32) | (lax.bitcast_convert_type(lo, U32) >> 16)


def _unpack_bf16_pair(word):
    lo = lax.bitcast_convert_type(word << 16, F32).astype(BF16)
    hi = lax.bitcast_convert_type(word & jnp.uint32(0xFFFF0000), F32).astype(BF16)
    return lo, hi


def _round_bf16(x):
    return x.astype(BF16).astype(F32)


def _chunk_recurrence(q, k, v, g, st_ref, tril, causal, bmask, heads, dk, dv):
    rows = q.shape[0]
    cum = tril.shape[0]
    gc = jnp.concatenate([_split_dot(tril, g[r:r + cum]) for r in range(0, rows, cum)], axis=0)
    lane_k = lax.broadcasted_iota(I32, (1, heads * dk), 1) // dk
    lane_v = lax.broadcasted_iota(I32, (1, heads * dv), 1) // dv
    st = st_ref[...]
    outs = []
    for c in range(rows // CHUNK):
        sl = slice(c * CHUNK, (c + 1) * CHUNK)
        gcc, qc, kc, vc = gc[sl], q[sl], k[sl], v[sl]
        mid = gcc[CHUNK // 2 - 1:CHUNK // 2]
        last = gcc[CHUNK - 1:CHUNK]
        qt = qc * jnp.exp(gcc - mid)
        kt = kc * jnp.exp(mid - gcc)
        qi = qc * jnp.exp(gcc)
        ks = kc * jnp.exp(last - gcc)
        vb = vc.astype(BF16)
        qstack = jnp.concatenate([jnp.where(lane_k == h, qt, 0.0) for h in range(heads)], axis=0)
        a = _dot_nt(qstack.astype(BF16), kt.astype(BF16))
        a = jnp.where(causal, a, 0.0)
        ab = a.astype(BF16)
        if dv % LANES == 0:
            o_intra = jnp.concatenate([_dot(ab[h * CHUNK:(h + 1) * CHUNK], vb[:, h * dv:(h + 1) * dv])
                                       for h in range(heads)], axis=-1)
        else:
            o_all = _dot(ab, vb)
            o_intra = jnp.where(lane_v == 0, o_all[0:CHUNK], 0.0)
            for h in range(1, heads):
                o_intra = o_intra + jnp.where(lane_v == h, o_all[h * CHUNK:(h + 1) * CHUNK], 0.0)
        o_inter = _dot_nt(qi.astype(BF16), st.astype(BF16))
        u = _dot_tn(vb, ks.astype(BF16))
        st = jnp.exp(last) * st + u * bmask
        outs.append(o_intra + o_inter)
    st_ref[...] = st
    return jnp.concatenate(outs, axis=0)


def _gated_head_rmsnorm(o, gate, g_tiled, heads, dv):
    lane_v = lax.broadcasted_iota(I32, (1, heads * dv), 1) // dv
    sq = o * o
    ms = jnp.zeros_like(o)
    for h in range(heads):
        m = lane_v == h
        ms = jnp.where(m, jnp.sum(jnp.where(m, sq, 0.0), axis=-1, keepdims=True), ms)
    ms = ms * (1.0 / dv)
    return o * lax.rsqrt(ms + RMS_EPS) * g_tiled * (gate * _sigmoid(gate))


def _mixer_kernel(x_ref, lng_ref, lnb_ref, win_ref, wexp_ref, bgate_ref, glag_ref, lbraw_ref, hgg_ref,
                  convw_ref, wout_ref, mg_ref, mb_ref, tril_ref, causal_ref, bmg_ref, bmh_ref,
                  o_ref, sg_ref, sh_ref, cc_ref, *, layer, pre_ln, alpha):
    @pl.when(pl.program_id(1) == 0)
    def _():
        sg_ref[...] = jnp.zeros_like(sg_ref)
        sh_ref[...] = jnp.zeros_like(sh_ref)
        cc_ref[...] = jnp.zeros_like(cc_ref)

    x = x_ref[...]
    h = _layer_norm(x, lng_ref[...], lnb_ref[...]) if pre_ln else x
    hb = h.astype(BF16)
    rows = x.shape[0]

    def proj(cols):
        return _dot(hb, win_ref[:, cols[0]:cols[0] + -(-(cols[1] - cols[0]) // LANES) * LANES])

    tril = tril_ref[...]
    causal = causal_ref[...] > 0.5

    q = proj(C_GQ) * (GLA_HEAD_K ** -0.5)
    lr = proj(C_GLR)
    lr_hi = lr.astype(BF16)
    lr_lo = (lr - lr_hi.astype(F32)).astype(BF16)
    by_hi = _dot(lr_hi, wexp_ref[...])
    gate_logits = (by_hi[:, :GLA_K_DIM] + by_hi[:, GLA_K_DIM:] + _dot(lr_lo, wexp_ref[:, 0:GLA_K_DIM])
                   + bgate_ref[...])
    log_a = (jnp.minimum(gate_logits, 0.0) - jnp.log(1.0 + jnp.exp(-jnp.abs(gate_logits)))) * (1.0 / GLA_GATE_NORMALIZER)
    o_gla = _chunk_recurrence(q, proj(C_GK), proj(C_GV), log_a, sg_ref, tril, causal, bmg_ref[...],
                              GLA_HEADS, GLA_HEAD_K, GLA_HEAD_V)
    y_gla = _gated_head_rmsnorm(o_gla, proj(C_GOG), glag_ref[...], GLA_HEADS, GLA_HEAD_V)

    raw = lbraw_ref[...]
    e = jnp.exp(raw - jnp.max(raw, axis=0, keepdims=True))
    lb = jnp.zeros_like(e[0:1])
    for i in range(1, layer + 1):
        lb = lb + e[i:i + 1]
    lb = lb / jnp.sum(e, axis=0, keepdims=True)
    f = lb + (1.0 - lb) * _sigmoid(proj(C_HF))
    hq = proj(C_HQ)
    o_h = _chunk_recurrence(hq * _sigmoid(hq), 1.0 - f, proj(C_HI), jnp.log(f), sh_ref, tril, causal,
                            bmh_ref[...], HGRN_HEADS, HGRN_HEAD_F, HGRN_HEAD_I)
    y_h = _gated_head_rmsnorm(o_h, proj(C_HOG), hgg_ref[...], HGRN_HEADS, HGRN_HEAD_I)

    u = proj(C_CC) * proj(C_CH)
    prev = cc_ref[...]
    row = lax.broadcasted_iota(I32, (rows, 1), 0)
    u1 = jnp.where(row == 0, prev[SUBLANES - 1:SUBLANES], pltpu.roll(u, 1, 0))
    u2 = jnp.where(row == 0, prev[SUBLANES - 2:SUBLANES - 1],
                   jnp.where(row == 1, prev[SUBLANES - 1:SUBLANES], pltpu.roll(u, 2, 0)))
    cc_ref[...] = u[rows - SUBLANES:rows]
    cw = convw_ref[...]
    y_conv = proj(C_CB) * (cw[0:1] * u2 + cw[1:2] * u1 + cw[2:3] * u)

    o1, o2 = GLA_V_DIM, GLA_V_DIM + HGRN_I_DIM
    mix = (_dot(y_gla.astype(BF16), wout_ref[0:o1, :]) + _dot(y_h.astype(BF16), wout_ref[o1:o2, :])
           + _dot(y_conv.astype(BF16), wout_ref[o2:o2 + CONV_CHANNELS, :]))
    o_ref[...] = _layer_norm(alpha * h + mix, mg_ref[...], mb_ref[...])


def _mixer(x, ln_g, ln_b, w_in_r, w_exp, b_gate, gla_g, lb_raw, hgrn_g, conv_w, w_out, mg, mb,
           consts, *, layer, pre_ln, alpha):
    bsz, t, d = x.shape
    rows = min(MIX_ROWS, t)
    tril, causal, bmg, bmh = consts
    assert t % rows == 0 and rows % tril.shape[0] == 0 and tril.shape[0] % CHUNK == 0

    def lay(shape):
        return pl.BlockSpec((None,) + shape, lambda b, i: (layer,) + (0,) * len(shape))

    def whole(a):
        return pl.BlockSpec(a.shape, lambda b, i: (0,) * a.ndim)

    return pl.pallas_call(
        functools.partial(_mixer_kernel, layer=layer, pre_ln=pre_ln, alpha=alpha),
        grid=(bsz, t // rows),
        in_specs=[
            pl.BlockSpec((None, rows, d), lambda b, i: (b, i, 0)),
            whole(ln_g), whole(ln_b),
            lay((d, D_IN_PAD)), lay((LANES, 2 * GLA_K_DIM)), lay((1, GLA_K_DIM)), lay((1, GLA_V_DIM)),
            whole(lb_raw), lay((1, HGRN_I_DIM)), lay((CONV_WIDTH, CONV_CHANNELS)), lay((d, d)),
            lay((1, d)), lay((1, d)),
            whole(tril), whole(causal), whole(bmg), whole(bmh),
        ],
        out_specs=pl.BlockSpec((None, rows, d), lambda b, i: (b, i, 0)),
        out_shape=jax.ShapeDtypeStruct((bsz, t, d), F32),
        scratch_shapes=[pltpu.VMEM((GLA_V_DIM, GLA_K_DIM), F32), pltpu.VMEM((HGRN_I_DIM, HGRN_F_DIM), F32),
                        pltpu.VMEM((SUBLANES, CONV_CHANNELS), F32)],
        compiler_params=pltpu.CompilerParams(dimension_semantics=("arbitrary", "arbitrary"),
                                             vmem_limit_bytes=VMEM_LIMIT_BYTES),
        name=f"mixer_l{layer}",
    )(x, ln_g, ln_b, w_in_r, w_exp, b_gate, gla_g, lb_raw, hgrn_g, conv_w, w_out, mg, mb, tril, causal, bmg, bmh)


def _mixer_consts(rows):
    r = jnp.arange(rows)
    tril = ((r[:, None] // CHUNK == r[None, :] // CHUNK) & (r[None, :] <= r[:, None])).astype(BF16)
    i = jnp.arange(GLA_HEADS * CHUNK) % CHUNK
    causal = (jnp.arange(CHUNK)[None, :] <= i[:, None]).astype(F32)
    bmg = (jnp.arange(GLA_V_DIM)[:, None] // GLA_HEAD_V == jnp.arange(GLA_K_DIM)[None, :] // GLA_HEAD_K).astype(F32)
    bmh = (jnp.arange(HGRN_I_DIM)[:, None] // HGRN_HEAD_I == jnp.arange(HGRN_F_DIM)[None, :] // HGRN_HEAD_F).astype(F32)
    return tril, causal, bmg, bmh


def _router_kernel(h_ref, wr_ref, br_ref, ut_ref, lt_ref, gate_ref, pos_ref, cnt_ref, off_ref, before_ref,
                   crow_ref, cexp_ref, ctot_ref, total_ref, carry_ref):
    @pl.when(pl.program_id(0) == 0)
    def _():
        carry_ref[...] = jnp.zeros_like(carry_ref)

    h = h_ref[...]
    h_hi = h.astype(BF16)
    h_lo = (h - h_hi.astype(F32)).astype(BF16)
    w = wr_ref[...]
    by_hi = _dot_nt(w, h_hi)
    l = by_hi[0:N_EXPERTS] + by_hi[N_EXPERTS:2 * N_EXPERTS] + _dot_nt(w[0:N_EXPERTS], h_lo) + br_ref[...]
    expert = lax.broadcasted_iota(I32, l.shape, 0).astype(F32)
    neg = jnp.float32(-jnp.inf)
    tops, idxs = [], []
    for _ in range(TOP_K):
        m = jnp.max(l, axis=0, keepdims=True)
        ik = jnp.min(jnp.where(l == m, expert, float(N_EXPERTS)), axis=0, keepdims=True)
        tops.append(m)
        idxs.append(ik)
        l = jnp.where(expert == ik, neg, l)
    es = [jnp.exp(t - tops[0]) for t in tops]
    denom = es[0] + es[1] + es[2] + es[3]
    onehot = jnp.zeros_like(l)
    for ik in idxs:
        onehot = onehot + jnp.where(expert == ik, 1.0, 0.0)

    carry = carry_ref[...]
    earlier = _dot(onehot.astype(BF16), ut_ref[...])
    cnt = jnp.broadcast_to(jnp.sum(onehot, axis=1, keepdims=True), carry.shape)
    phase = carry - PERM_GRAN * jnp.floor(carry * (1.0 / PERM_GRAN))
    run = jnp.where(cnt > 0.0, jnp.ceil((phase + cnt) * (1.0 / PERM_CHUNK)) * PERM_CHUNK, 0.0)
    off = _dot_f32(lt_ref[...], run)
    local = earlier + (off + phase)[:, 0:1]
    pos = [jnp.sum(jnp.where(expert == ik, local, 0.0), axis=0, keepdims=True) for ik in idxs]

    n_chunks = run * (1.0 / PERM_CHUNK)
    first_chunk = off * (1.0 / PERM_CHUNK)
    chunk = lax.broadcasted_iota(I32, carry.shape, 1).astype(F32)
    owner = lax.broadcasted_iota(I32, carry.shape, 0).astype(F32)
    owns = (chunk >= first_chunk) & (chunk < first_chunk + n_chunks)
    row0 = jnp.where(owns, (carry - phase) + PERM_CHUNK * (chunk - first_chunk), 0.0)
    list_shape = crow_ref.shape
    crow_ref[...] = jnp.broadcast_to(jnp.sum(row0, axis=0, keepdims=True), list_shape).astype(I32)
    cexp_ref[...] = jnp.broadcast_to(jnp.sum(jnp.where(owns, owner, 0.0), axis=0, keepdims=True), list_shape).astype(I32)
    ctot_ref[...] = jnp.broadcast_to(jnp.sum(n_chunks, axis=0, keepdims=True), list_shape).astype(I32)

    cnt_ref[...] = cnt.astype(I32)
    off_ref[...] = off.astype(I32)
    before_ref[...] = carry.astype(I32)
    carry = carry + cnt
    carry_ref[...] = carry
    total_ref[...] = carry.astype(I32)
    gate_ref[...] = _rows_to_sublanes([e / denom for e in es])
    pos_ref[...] = _rows_to_sublanes(pos).astype(I32)


def _router(h, w_router_p, b_router_p, layer):
    n, d = h.shape
    rows = min(PERM_ROWS, n)
    assert n % rows == 0
    nblk = n // rows
    r = jnp.arange(rows)
    upper = (r[:, None] < r[None, :]).astype(BF16)
    e = jnp.arange(N_EXPERTS)
    lower = (e[None, :] < e[:, None]).astype(F32)
    pair = pl.BlockSpec((TOP_K, rows), lambda i: (0, i))
    table = pl.BlockSpec((None, N_EXPERTS, LANES), lambda i: (i, 0, 0))
    table_shape = jax.ShapeDtypeStruct((nblk, N_EXPERTS, LANES), I32)
    assert PERM_LOCAL // PERM_CHUNK <= LANES
    chunk_list = pl.BlockSpec((None, SUBLANES, LANES), lambda i: (i, 0, 0))
    list_shape = jax.ShapeDtypeStruct((nblk, SUBLANES, LANES), I32)
    return pl.pallas_call(
        _router_kernel,
        grid=(nblk,),
        in_specs=[pl.BlockSpec((rows, d), lambda i: (i, 0)),
                  pl.BlockSpec((None, 2 * N_EXPERTS, d), lambda i: (layer, 0, 0)),
                  pl.BlockSpec((None, N_EXPERTS, 1), lambda i: (layer, 0, 0)),
                  pl.BlockSpec((rows, rows), lambda i: (0, 0)),
                  pl.BlockSpec((N_EXPERTS, N_EXPERTS), lambda i: (0, 0))],
        out_specs=[pair, pair, table, table, table, chunk_list, chunk_list, chunk_list,
                   pl.BlockSpec((N_EXPERTS, LANES), lambda i: (0, 0))],
        out_shape=[jax.ShapeDtypeStruct((TOP_K, n), F32), jax.ShapeDtypeStruct((TOP_K, n), I32),
                   table_shape, table_shape, table_shape, list_shape, list_shape, list_shape,
                   jax.ShapeDtypeStruct((N_EXPERTS, LANES), I32)],
        scratch_shapes=[pltpu.VMEM((N_EXPERTS, LANES), F32)],
        compiler_params=pltpu.CompilerParams(dimension_semantics=("arbitrary",), vmem_limit_bytes=VMEM_LIMIT_BYTES),
        name=f"router_l{layer}",
    )(h, w_router_p, b_router_p, upper, lower)


def _for_each_chunk(n_chunks, crow_ref, table_row, chunk_fn):
    base = table_row * LANES

    def body(j, carry):
        chunk_fn(pl.multiple_of(j * PERM_CHUNK, PERM_CHUNK), pl.multiple_of(crow_ref[base + j], PERM_GRAN))
        return carry

    lax.fori_loop(0, n_chunks, body, 0)


def _dispatch_kernel(pend_ref, pcnt_ref, cnt_ref, off_ref, before_ref, crow_ref, ctot_ref, h_ref, pos_ref, xs_ref,
                     xc_ref, zbuf_ref, tail_ref, tot_ref, sem, zsem, *, block_rows):
    i = pl.program_id(0)
    slot = i % 2

    def fill(first_row):
        start = pl.multiple_of(first_row, block_rows)
        return pltpu.make_async_copy(zbuf_ref, xs_ref.at[pl.ds(start, block_rows)], zsem)

    @pl.when(i == 0)
    def _():
        zbuf_ref[...] = jnp.zeros_like(zbuf_ref)
        tail_ref[...] = jnp.zeros_like(tail_ref)
        used_blocks = pend_ref[N_EXPERTS - 1] // block_rows
        all_blocks = xs_ref.shape[0] // block_rows
        for e in range(N_EXPERTS):
            @pl.when(pcnt_ref[e] > 0)
            def _():
                fill(pend_ref[e] - block_rows).start()
        lax.fori_loop(used_blocks, all_blocks, lambda j, c: (fill(j * block_rows).start(), c)[1], 0)
        for e in range(N_EXPERTS):
            @pl.when(pcnt_ref[e] > 0)
            def _():
                fill(pend_ref[e] - block_rows).wait()
        lax.fori_loop(used_blocks, all_blocks, lambda j, c: (fill(j * block_rows).wait(), c)[1], 0)

    pos = pos_ref[...]
    local_row = lax.broadcasted_iota(I32, (PERM_LOCAL, pos.shape[1]), 0)
    p = jnp.where(local_row == pos[0:1], 1.0, 0.0)
    for k in range(1, TOP_K):
        p = p + jnp.where(local_row == pos[k:k + 1], 1.0, 0.0)
    pb = p.astype(BF16)
    hb = h_ref[...].astype(BF16)
    half = hb.shape[1] // 2
    xc_ref[slot] = _pack_bf16_pair(_dot(pb, hb[:, :half]), _dot(pb, hb[:, half:]))

    def copy(s, local_row, global_row, size):
        return pltpu.make_async_copy(xc_ref.at[s, pl.ds(local_row, size)],
                                     xs_ref.at[pl.ds(global_row, size)], sem.at[s])

    def drain(s):
        @pl.when(tot_ref[0] > 0)
        def _():
            copy(s, 0, 0, tot_ref[0] * PERM_CHUNK).wait()

    @pl.when(i > 0)
    def _():
        drain(1 - slot)

    row8 = lax.broadcasted_iota(I32, (PERM_GRAN, 1), 0)

    def carry_partial_group(e, n, loc, phase):
        first = pl.ds(pl.multiple_of(loc, PERM_GRAN), PERM_GRAN)
        merged = jnp.where((row8 < phase) & (n > 0), tail_ref[e], xc_ref[slot, first, :])
        xc_ref[slot, first, :] = merged
        end = pl.ds(pl.multiple_of(loc + ((phase + n) & -PERM_GRAN), PERM_GRAN), PERM_GRAN)
        tail_ref[e] = jnp.where(n > 0, xc_ref[slot, end, :], tail_ref[e])

    base = (i % TABLE_GROUP) * LANES
    for e in range(N_EXPERTS):
        carry_partial_group(e, cnt_ref[base + e], off_ref[base + e], before_ref[base + e] & (PERM_GRAN - 1))
    tot_ref[0] = ctot_ref[i % SMEM_TABLE]
    _for_each_chunk(tot_ref[0], crow_ref, i % TABLE_GROUP, lambda lr, gr: copy(slot, lr, gr, PERM_CHUNK).start())

    @pl.when(i == pl.num_programs(0) - 1)
    def _():
        drain(slot)


def _table_spec(shift, nblk, per_block=LANES):
    group = SMEM_TABLE // per_block
    return pl.BlockSpec((SMEM_TABLE,), lambda i, *_: (jnp.minimum(i + shift, nblk - 1) // group,),
                        memory_space=pltpu.SMEM)


def _dispatch(h, pos, cnt_tbl, off_tbl, before_tbl, crow_tbl, ctot_tbl, pend, pcnt, total_rows, layer):
    n, d = h.shape
    rows = min(PERM_ROWS, n)
    nblk = n // rows
    return pl.pallas_call(
        functools.partial(_dispatch_kernel, block_rows=EXPERT_ROWS),
        grid_spec=pltpu.PrefetchScalarGridSpec(
            num_scalar_prefetch=2,
            grid=(nblk,),
            in_specs=[_table_spec(0, nblk), _table_spec(0, nblk), _table_spec(0, nblk), _table_spec(0, nblk),
                      _table_spec(0, nblk, per_block=1),
                      pl.BlockSpec((rows, d), lambda i, *_: (i, 0)),
                      pl.BlockSpec((TOP_K, rows), lambda i, *_: (0, i))],
            out_specs=pl.BlockSpec(memory_space=pl.ANY),
            scratch_shapes=[pltpu.VMEM((2, PERM_LOCAL, d // 2), U32), pltpu.VMEM((EXPERT_ROWS, d // 2), U32),
                            pltpu.VMEM((N_EXPERTS, PERM_GRAN, d // 2), U32), pltpu.SMEM((1,), I32),
                            pltpu.SemaphoreType.DMA((2,)), pltpu.SemaphoreType.DMA(())],
        ),
        out_shape=jax.ShapeDtypeStruct((total_rows, d // 2), U32),
        compiler_params=pltpu.CompilerParams(dimension_semantics=("arbitrary",), vmem_limit_bytes=VMEM_LIMIT_BYTES),
        name=f"dispatch_l{layer}",
    )(pend, pcnt, cnt_tbl, off_tbl, before_tbl, crow_tbl, ctot_tbl, h, pos)


def _combine_kernel(crow_ref, ctot_ref, ncrow_ref, nctot_ref, h_ref, gate_ref, pos_ref,
                    lg_ref, lb_ref, ys_ref, o_ref, yc_ref, sem, *, alpha):
    i = pl.program_id(0)
    slot = i % 2

    def copy(s, local_row, global_row, size):
        return pltpu.make_async_copy(ys_ref.at[pl.ds(global_row, size)],
                                     yc_ref.at[s, pl.ds(local_row, size)], sem.at[s])

    @pl.when(i == 0)
    def _():
        yc_ref[...] = jnp.zeros_like(yc_ref)
        _for_each_chunk(ctot_ref[0], crow_ref, 0, lambda lr, gr: copy(0, lr, gr, PERM_CHUNK).start())

    @pl.when(i + 1 < pl.num_programs(0))
    def _():
        _for_each_chunk(nctot_ref[(i + 1) % SMEM_TABLE], ncrow_ref, (i + 1) % TABLE_GROUP,
                        lambda lr, gr: copy(1 - slot, lr, gr, PERM_CHUNK).start())

    n_chunks = ctot_ref[i % SMEM_TABLE]

    @pl.when(n_chunks > 0)
    def _():
        copy(slot, 0, 0, n_chunks * PERM_CHUNK).wait()

    pos = pos_ref[...]
    gates = gate_ref[...]
    local_row = lax.broadcasted_iota(I32, (PERM_LOCAL, pos.shape[1]), 0)
    q = jnp.where(local_row == pos[0:1], gates[0:1], 0.0)
    for k in range(1, TOP_K):
        q = q + jnp.where(local_row == pos[k:k + 1], gates[k:k + 1], 0.0)
    qb = q.astype(BF16)
    y_lo, y_hi = _unpack_bf16_pair(yc_ref[slot])
    ffn = jnp.concatenate([_dot_tn(qb, y_lo), _dot_tn(qb, y_hi)], axis=-1)
    o_ref[...] = _layer_norm(alpha * h_ref[...] + ffn, lg_ref[...], lb_ref[...])


def _combine(h, gates, pos, crow_tbl, ctot_tbl, ys, ln_g, ln_b, layer, alpha):
    n, d = h.shape
    rows = min(PERM_ROWS, n)
    nblk = n // rows
    return pl.pallas_call(
        functools.partial(_combine_kernel, alpha=alpha),
        grid=(nblk,),
        in_specs=[_table_spec(0, nblk), _table_spec(0, nblk, per_block=1),
                  _table_spec(1, nblk), _table_spec(1, nblk, per_block=1),
                  pl.BlockSpec((rows, d), lambda i: (i, 0)),
                  pl.BlockSpec((TOP_K, rows), lambda i: (0, i)),
                  pl.BlockSpec((TOP_K, rows), lambda i: (0, i)),
                  pl.BlockSpec((None, 1, d), lambda i: (layer, 0, 0)),
                  pl.BlockSpec((None, 1, d), lambda i: (layer, 0, 0)),
                  pl.BlockSpec(memory_space=pl.ANY)],
        out_specs=pl.BlockSpec((rows, d), lambda i: (i, 0)),
        out_shape=jax.ShapeDtypeStruct((n, d), F32),
        scratch_shapes=[pltpu.VMEM((2, PERM_LOCAL, d // 2), U32), pltpu.SemaphoreType.DMA((2,))],
        compiler_params=pltpu.CompilerParams(dimension_semantics=("arbitrary",), vmem_limit_bytes=VMEM_LIMIT_BYTES),
        name=f"combine_l{layer}",
    )(crow_tbl, ctot_tbl, crow_tbl, ctot_tbl, h, gates, pos, ln_g, ln_b, ys)


def _expert_kernel(be_ref, nact_ref, x_ref, wgu_ref, bgu_ref, wd_ref, bd_ref, y_ref, wgu_b_ref, wd_b_ref):
    j = pl.program_id(0)
    d_ff = wd_ref.shape[0]
    half = y_ref.shape[1]

    @pl.when((j == 0) | (be_ref[j] != be_ref[jnp.maximum(j - 1, 0)]))
    def _():
        wgu_b_ref[...] = wgu_ref[...].astype(BF16)
        wd_b_ref[...] = wd_ref[...].astype(BF16)

    @pl.when(j < nact_ref[0])
    def _():
        x = jnp.concatenate(_unpack_bf16_pair(x_ref[...]), axis=-1)
        gu = _dot(x, wgu_b_ref[...]) + bgu_ref[...]
        gate = jnp.minimum(gu[:, :d_ff], SWIGLU_LIMIT)
        up = jnp.clip(gu[:, d_ff:], -SWIGLU_LIMIT, SWIGLU_LIMIT)
        act = (up + 1.0) * gate * _sigmoid(SWIGLU_ALPHA * gate)
        y = _dot(act.astype(BF16), wd_b_ref[...]) + bd_ref[...]
        y_ref[...] = _pack_bf16_pair(_round_bf16(y[:, :half]), _round_bf16(y[:, half:]))

    @pl.when(j >= nact_ref[0])
    def _():
        y_ref[...] = jnp.zeros_like(y_ref)


def _experts(xs, blk_expert, nact, wgu, bgu, wd, bd, layer):
    total_rows, half = xs.shape
    nblk = total_rows // EXPERT_ROWS
    d = wgu.shape[2]
    d_ff = wd.shape[2]

    def row_block(j, be, na):
        return (j, 0)

    def of_expert(j, be, na):
        return (layer, be[j], 0, 0)

    return pl.pallas_call(
        _expert_kernel,
        grid_spec=pltpu.PrefetchScalarGridSpec(
            num_scalar_prefetch=2,
            grid=(nblk,),
            in_specs=[pl.BlockSpec((EXPERT_ROWS, half), row_block),
                      pl.BlockSpec((None, None, d, 2 * d_ff), of_expert),
                      pl.BlockSpec((None, None, 1, 2 * d_ff), of_expert),
                      pl.BlockSpec((None, None, d_ff, d), of_expert),
                      pl.BlockSpec((None, None, 1, d), of_expert)],
            out_specs=pl.BlockSpec((EXPERT_ROWS, half), row_block),
            scratch_shapes=[pltpu.VMEM((d, 2 * d_ff), BF16), pltpu.VMEM((d_ff, d), BF16)],
        ),
        out_shape=jax.ShapeDtypeStruct((total_rows, half), U32),
        compiler_params=pltpu.CompilerParams(dimension_semantics=("arbitrary",), vmem_limit_bytes=VMEM_LIMIT_BYTES),
        name=f"experts_l{layer}",
    )(blk_expert, nact, xs, wgu, bgu, wd, bd)


def _moe(h, w_router_p, b_router_p, wgu, bgu, wd, bd, ln_g, ln_b, layer, alpha):
    n, d = h.shape
    (gates, pos, cnt_tbl, off_tbl, before_tbl, crow_tbl, cexp_tbl, ctot_tbl,
     totals) = _router(h, w_router_p, b_router_p, layer)
    counts = totals[:, 0]
    padded = (counts + PERM_CHUNK + EXPERT_ROWS - 1) // EXPERT_ROWS * EXPERT_ROWS
    padded = jnp.where(counts > 0, padded, 0)
    pend = jnp.cumsum(padded).astype(I32)
    pstart = pend - padded
    nblk = (n * TOP_K + N_EXPERTS * PERM_CHUNK) // EXPERT_ROWS + N_EXPERTS
    nact = (pend[-1] // EXPERT_ROWS).reshape(1).astype(I32)
    blk = jnp.minimum(jnp.arange(nblk, dtype=I32), nact[0] - 1) * EXPERT_ROWS
    blk_expert = jnp.minimum(jnp.sum(pend[None, :] <= blk[:, None], axis=1), N_EXPERTS - 1).astype(I32)

    def flat(tbl):
        tbl = tbl.reshape(-1)
        return jnp.pad(tbl, (0, -tbl.shape[0] % SMEM_TABLE))

    def per_expert(tbl):
        return flat(jnp.pad(tbl[:, :, 0], ((0, 0), (0, LANES - N_EXPERTS))))

    cexp = cexp_tbl[:, 0, :]
    region = jnp.sum(jnp.where(cexp[:, :, None] == jnp.arange(N_EXPERTS)[None, None, :], pstart[None, None, :], 0), axis=-1)
    crow_f = flat(crow_tbl[:, 0, :] + region)
    ctot_f = flat(ctot_tbl[:, 0, 0])

    xs = _dispatch(h, pos, per_expert(cnt_tbl), per_expert(off_tbl), per_expert(before_tbl), crow_f, ctot_f,
                   pend, padded.astype(I32), nblk * EXPERT_ROWS, layer)
    ys = _experts(xs, blk_expert, nact, wgu, bgu, wd, bd, layer)
    return _combine(h, gates, pos, crow_f, ctot_f, ys, ln_g, ln_b, layer, alpha)


def _relayout_w_in(w_in):
    depth, d, _ = w_in.shape
    out = jnp.zeros((depth, d, D_IN_PAD), BF16)
    for g in range(len(IN_SPLITS)):
        lo, hi = _COLS[g]
        out = out.at[:, :, lo:hi].set(w_in[:, :, _SRC_OFF[g]:_SRC_OFF[g] + IN_SPLITS[g]].astype(BF16))
    return out


def kernel(x, ln_in_g, ln_in_b, w_in, w_gla_gate_expand, b_gla_gate, gla_norm_g, hgrn_lb_raw, hgrn_norm_g, conv_w, w_out, ln_mix_g, ln_mix_b, w_router, b_router, w_gate_up, b_gate_up, w_down, b_down, ln_ffn_g, ln_ffn_b):
    bsz, t, d = x.shape
    depth = w_in.shape[0]
    alpha = (2 * depth) ** 0.25
    n = bsz * t

    w_in_r = _relayout_w_in(w_in)
    w_exp = jnp.zeros((depth, LANES, GLA_K_DIM), F32).at[:, :GLA_GATE_RANK, :].set(w_gla_gate_expand)
    w_exp_hi = w_exp.astype(BF16)
    w_exp = jnp.concatenate([w_exp_hi, (w_exp - w_exp_hi.astype(F32)).astype(BF16)], axis=-1)
    b_gate = b_gla_gate.reshape(depth, 1, GLA_K_DIM)
    gla_g = jnp.tile(gla_norm_g, (1, GLA_HEADS)).reshape(depth, 1, GLA_V_DIM)
    hgrn_g = jnp.tile(hgrn_norm_g, (1, HGRN_HEADS)).reshape(depth, 1, HGRN_I_DIM)
    w_out_b = w_out.astype(BF16)
    w_router_t = jnp.swapaxes(w_router, 1, 2)
    w_router_hi = w_router_t.astype(BF16)
    w_router_lo = (w_router_t - w_router_hi.astype(F32)).astype(BF16)
    w_router_p = jnp.concatenate([w_router_hi, w_router_lo], axis=1)
    b_router_p = b_router.reshape(depth, N_EXPERTS, 1)
    wgu, wd = w_gate_up, w_down
    bgu = b_gate_up.reshape(depth, N_EXPERTS, 1, -1)
    bd = b_down.reshape(depth, N_EXPERTS, 1, -1)
    row = lambda a: a.reshape(depth, 1, d)
    consts = _mixer_consts(min(CUMSUM_ROWS, t))

    h = x
    for l in range(depth):
        h = _mixer(h, ln_in_g.reshape(1, d), ln_in_b.reshape(1, d), w_in_r, w_exp, b_gate, gla_g, hgrn_lb_raw,
                   hgrn_g, conv_w, w_out_b, row(ln_mix_g), row(ln_mix_b), consts,
                   layer=l, pre_ln=(l == 0), alpha=alpha)
        h = _moe(h.reshape(n, d), w_router_p, b_router_p, wgu, bgu, wd, bd, row(ln_ffn_g), row(ln_ffn_b),
                 l, alpha).reshape(bsz, t, d)
    return h
```

```python
import functools

import jax
import jax.numpy as jnp
from jax import lax
from jax.experimental import pallas as pl
from jax.experimental.pallas import tpu as pltpu

F32 = jnp.float32
BF16 = jnp.bfloat16
U32 = jnp.uint32
I32 = jnp.int32

GLA_HEADS = 4
GLA_HEAD_K = 64
GLA_HEAD_V = 128
GLA_GATE_RANK = 16
GLA_GATE_NORMALIZER = 16.0
HGRN_HEADS = 4
HGRN_HEAD_F = 64
HGRN_HEAD_I = 64
CONV_CHANNELS = 256
CONV_WIDTH = 3
CHUNK = 64
N_EXPERTS = 32
TOP_K = 4
SWIGLU_LIMIT = 7.0
SWIGLU_ALPHA = 1.702
LN_EPS = 1e-5
RMS_EPS = 1e-6

GLA_K_DIM = GLA_HEADS * GLA_HEAD_K
GLA_V_DIM = GLA_HEADS * GLA_HEAD_V
HGRN_F_DIM = HGRN_HEADS * HGRN_HEAD_F
HGRN_I_DIM = HGRN_HEADS * HGRN_HEAD_I
IN_SPLITS = (GLA_K_DIM, GLA_K_DIM, GLA_V_DIM, GLA_GATE_RANK, GLA_V_DIM,
             HGRN_F_DIM, HGRN_F_DIM, HGRN_I_DIM, HGRN_I_DIM,
             CONV_CHANNELS, CONV_CHANNELS, CONV_CHANNELS)

LANES = 128
SUBLANES = 8
VMEM_LIMIT_BYTES = 56 * 1024 * 1024

MIX_ROWS = 1024
CUMSUM_ROWS = 256
PERM_ROWS = 256
PERM_GRAN = SUBLANES
PERM_CHUNK = 16
PERM_LOCAL = -(-(PERM_ROWS * TOP_K + N_EXPERTS * (PERM_GRAN + PERM_CHUNK - 2)) // LANES) * LANES
EXPERT_ROWS = 512
SMEM_TABLE = 1024
TABLE_GROUP = SMEM_TABLE // LANES

_ORDER = (0, 1, 2, 4, 5, 6, 7, 8, 9, 10, 11, 3)
_SRC_OFF = [sum(IN_SPLITS[:i]) for i in range(len(IN_SPLITS))]
_COLS = {}
_off = 0
for _g in _ORDER:
    _w = IN_SPLITS[_g]
    _COLS[_g] = (_off, _off + _w)
    _off += -(-_w // LANES) * LANES
D_IN_PAD = _off
(C_GQ, C_GK, C_GV, C_GLR, C_GOG, C_HQ, C_HF, C_HI, C_HOG, C_CB, C_CC, C_CH) = (_COLS[i] for i in range(12))


def _layer_norm(x, g, b):
    mu = jnp.mean(x, axis=-1, keepdims=True)
    xc = x - mu
    var = jnp.mean(xc * xc, axis=-1, keepdims=True)
    return xc * lax.rsqrt(var + LN_EPS) * g + b


def _sigmoid(x):
    return 1.0 / (1.0 + jnp.exp(-x))


def _dot(a, b):
    return jnp.dot(a, b, preferred_element_type=F32)


def _dot_nt(a, b):
    return lax.dot_general(a, b, (((1,), (1,)), ((), ())), preferred_element_type=F32)


def _dot_tn(a, b):
    return lax.dot_general(a, b, (((0,), (0,)), ((), ())), preferred_element_type=F32)


def _dot_f32(a, b):
    return lax.dot_general(a, b, (((1,), (0,)), ((), ())), precision=lax.Precision.HIGHEST,
                           preferred_element_type=F32)


def _split_dot(a01, x):
    hi = x.astype(BF16)
    lo = (x - hi.astype(F32)).astype(BF16)
    return _dot(a01, hi) + _dot(a01, lo)


def _rows_to_sublanes(rows):
    sub = lax.broadcasted_iota(I32, (len(rows), rows[0].shape[1]), 0)
    out = jnp.broadcast_to(rows[-1], sub.shape)
    for k in range(len(rows) - 2, -1, -1):
        out = jnp.where(sub == k, rows[k], out)
    return out


def _pack_bf16_pair(lo, hi):
    return lax.bitcast_convert_type(hi, U32) | (lax.bitcast_convert_type(lo, U32) >> 16)


def _unpack_bf16_pair(word):
    lo = lax.bitcast_convert_type(word << 16, F32).astype(BF16)
    hi = lax.bitcast_convert_type(word & jnp.uint32(0xFFFF0000), F32).astype(BF16)
    return lo, hi


def _round_bf16(x):
    return x.astype(BF16).astype(F32)


def _chunk_scores(q, k, gc, lane_k, heads, causal):
    half = CHUNK // 2
    row = lax.broadcasted_iota(I32, (CHUNK, 1), 0)
    second = row >= half
    mid = jnp.where(second, gc[half + half // 2 - 1:half + half // 2], gc[half // 2 - 1:half // 2])
    q_same = q * jnp.exp(gc - mid)
    k_same = k * jnp.exp(mid - gc)
    edge = gc[half - 1:half]
    q_cross = q[half:] * jnp.exp(gc[half:] - edge)
    k_cross = jnp.where(second, 0.0, k * jnp.exp(edge - gc))

    def stack(x):
        return jnp.concatenate([jnp.where(lane_k == h, x, 0.0) for h in range(heads)], axis=0).astype(BF16)

    same = _dot_nt(stack(q_same), k_same.astype(BF16))
    cross = _dot_nt(stack(q_cross), k_cross.astype(BF16))
    none = jnp.zeros((half, CHUNK), F32)
    cross = jnp.concatenate([blk for h in range(heads) for blk in (none, cross[h * half:(h + 1) * half])], axis=0)
    return jnp.where(causal, same, 0.0) + cross


def _chunk_recurrence(q, k, v, g, st_ref, tril, causal, bmask, heads, dk, dv):
    rows = q.shape[0]
    cum = tril.shape[0]
    gc = jnp.concatenate([_split_dot(tril, g[r:r + cum]) for r in range(0, rows, cum)], axis=0)
    lane_k = lax.broadcasted_iota(I32, (1, heads * dk), 1) // dk
    lane_v = lax.broadcasted_iota(I32, (1, heads * dv), 1) // dv
    st = st_ref[...]
    outs = []
    for c in range(rows // CHUNK):
        sl = slice(c * CHUNK, (c + 1) * CHUNK)
        gcc, qc, kc, vc = gc[sl], q[sl], k[sl], v[sl]
        last = gcc[CHUNK - 1:CHUNK]
        qi = qc * jnp.exp(gcc)
        ks = kc * jnp.exp(last - gcc)
        vb = vc.astype(BF16)
        ab = _chunk_scores(qc, kc, gcc, lane_k, heads, causal).astype(BF16)
        if dv % LANES == 0:
            o_intra = jnp.concatenate([_dot(ab[h * CHUNK:(h + 1) * CHUNK], vb[:, h * dv:(h + 1) * dv])
                                       for h in range(heads)], axis=-1)
        else:
            o_all = _dot(ab, vb)
            o_intra = jnp.where(lane_v == 0, o_all[0:CHUNK], 0.0)
            for h in range(1, heads):
                o_intra = o_intra + jnp.where(lane_v == h, o_all[h * CHUNK:(h + 1) * CHUNK], 0.0)
        o_inter = _dot_nt(qi.astype(BF16), st.astype(BF16))
        u = _dot_tn(vb, ks.astype(BF16))
        st = jnp.exp(last) * st + u * bmask
        outs.append(o_intra + o_inter)
    st_ref[...] = st
    return jnp.concatenate(outs, axis=0)


def _gated_head_rmsnorm(o, gate, g_tiled, heads, dv):
    lane_v = lax.broadcasted_iota(I32, (1, heads * dv), 1) // dv
    sq = o * o
    ms = jnp.zeros_like(o)
    for h in range(heads):
        m = lane_v == h
        ms = jnp.where(m, jnp.sum(jnp.where(m, sq, 0.0), axis=-1, keepdims=True), ms)
    ms = ms * (1.0 / dv)
    return o * lax.rsqrt(ms + RMS_EPS) * g_tiled * (gate * _sigmoid(gate))


def _mixer_kernel(x_ref, lng_ref, lnb_ref, win_ref, wexp_ref, bgate_ref, glag_ref, lbraw_ref, hgg_ref,
                  convw_ref, wout_ref, mg_ref, mb_ref, tril_ref, causal_ref, bmg_ref, bmh_ref,
                  o_ref, sg_ref, sh_ref, cc_ref, *, layer, pre_ln, alpha):
    @pl.when(pl.program_id(1) == 0)
    def _():
        sg_ref[...] = jnp.zeros_like(sg_ref)
        sh_ref[...] = jnp.zeros_like(sh_ref)
        cc_ref[...] = jnp.zeros_like(cc_ref)

    x = x_ref[...]
    h = _layer_norm(x, lng_ref[...], lnb_ref[...]) if pre_ln else x
    hb = h.astype(BF16)
    rows = x.shape[0]

    def proj(cols):
        return _dot(hb, win_ref[:, cols[0]:cols[0] + -(-(cols[1] - cols[0]) // LANES) * LANES])

    tril = tril_ref[...]
    causal = causal_ref[...] > 0.5

    q = proj(C_GQ) * (GLA_HEAD_K ** -0.5)
    lr = proj(C_GLR)
    lr_hi = lr.astype(BF16)
    lr_lo = (lr - lr_hi.astype(F32)).astype(BF16)
    by_hi = _dot(lr_hi, wexp_ref[...])
    gate_logits = (by_hi[:, :GLA_K_DIM] + by_hi[:, GLA_K_DIM:] + _dot(lr_lo, wexp_ref[:, 0:GLA_K_DIM])
                   + bgate_ref[...])
    log_a = (jnp.minimum(gate_logits, 0.0) - jnp.log(1.0 + jnp.exp(-jnp.abs(gate_logits)))) * (1.0 / GLA_GATE_NORMALIZER)
    o_gla = _chunk_recurrence(q, proj(C_GK), proj(C_GV), log_a, sg_ref, tril, causal, bmg_ref[...],
                              GLA_HEADS, GLA_HEAD_K, GLA_HEAD_V)
    y_gla = _gated_head_rmsnorm(o_gla, proj(C_GOG), glag_ref[...], GLA_HEADS, GLA_HEAD_V)

    raw = lbraw_ref[...]
    e = jnp.exp(raw - jnp.max(raw, axis=0, keepdims=True))
    lb = jnp.zeros_like(e[0:1])
    for i in range(1, layer + 1):
        lb = lb + e[i:i + 1]
    lb = lb / jnp.sum(e, axis=0, keepdims=True)
    f = lb + (1.0 - lb) * _sigmoid(proj(C_HF))
    hq = proj(C_HQ)
    o_h = _chunk_recurrence(hq * _sigmoid(hq), 1.0 - f, proj(C_HI), jnp.log(f), sh_ref, tril, causal,
                            bmh_ref[...], HGRN_HEADS, HGRN_HEAD_F, HGRN_HEAD_I)
    y_h = _gated_head_rmsnorm(o_h, proj(C_HOG), hgg_ref[...], HGRN_HEADS, HGRN_HEAD_I)

    u = proj(C_CC) * proj(C_CH)
    prev = cc_ref[...]
    row = lax.broadcasted_iota(I32, (rows, 1), 0)
    u1 = jnp.where(row == 0, prev[SUBLANES - 1:SUBLANES], pltpu.roll(u, 1, 0))
    u2 = jnp.where(row == 0, prev[SUBLANES - 2:SUBLANES - 1],
                   jnp.where(row == 1, prev[SUBLANES - 1:SUBLANES], pltpu.roll(u, 2, 0)))
    cc_ref[...] = u[rows - SUBLANES:rows]
    cw = convw_ref[...]
    y_conv = proj(C_CB) * (cw[0:1] * u2 + cw[1:2] * u1 + cw[2:3] * u)

    o1, o2 = GLA_V_DIM, GLA_V_DIM + HGRN_I_DIM
    mix = (_dot(y_gla.astype(BF16), wout_ref[0:o1, :]) + _dot(y_h.astype(BF16), wout_ref[o1:o2, :])
           + _dot(y_conv.astype(BF16), wout_ref[o2:o2 + CONV_CHANNELS, :]))
    o_ref[...] = _layer_norm(alpha * h + mix, mg_ref[...], mb_ref[...])


def _mixer(x, ln_g, ln_b, w_in_r, w_exp, b_gate, gla_g, lb_raw, hgrn_g, conv_w, w_out, mg, mb,
           consts, *, layer, pre_ln, alpha):
    bsz, t, d = x.shape
    rows = min(MIX_ROWS, t)
    tril, causal, bmg, bmh = consts
    assert t % rows == 0 and rows % tril.shape[0] == 0 and tril.shape[0] % CHUNK == 0

    def lay(shape):
        return pl.BlockSpec((None,) + shape, lambda b, i: (layer,) + (0,) * len(shape))

    def whole(a):
        return pl.BlockSpec(a.shape, lambda b, i: (0,) * a.ndim)

    return pl.pallas_call(
        functools.partial(_mixer_kernel, layer=layer, pre_ln=pre_ln, alpha=alpha),
        grid=(bsz, t // rows),
        in_specs=[
            pl.BlockSpec((None, rows, d), lambda b, i: (b, i, 0)),
            whole(ln_g), whole(ln_b),
            lay((d, D_IN_PAD)), lay((LANES, 2 * GLA_K_DIM)), lay((1, GLA_K_DIM)), lay((1, GLA_V_DIM)),
            whole(lb_raw), lay((1, HGRN_I_DIM)), lay((CONV_WIDTH, CONV_CHANNELS)), lay((d, d)),
            lay((1, d)), lay((1, d)),
            whole(tril), whole(causal), whole(bmg), whole(bmh),
        ],
        out_specs=pl.BlockSpec((None, rows, d), lambda b, i: (b, i, 0)),
        out_shape=jax.ShapeDtypeStruct((bsz, t, d), F32),
        scratch_shapes=[pltpu.VMEM((GLA_V_DIM, GLA_K_DIM), F32), pltpu.VMEM((HGRN_I_DIM, HGRN_F_DIM), F32),
                        pltpu.VMEM((SUBLANES, CONV_CHANNELS), F32)],
        compiler_params=pltpu.CompilerParams(dimension_semantics=("arbitrary", "arbitrary"),
                                             vmem_limit_bytes=VMEM_LIMIT_BYTES),
        name=f"mixer_l{layer}",
    )(x, ln_g, ln_b, w_in_r, w_exp, b_gate, gla_g, lb_raw, hgrn_g, conv_w, w_out, mg, mb, tril, causal, bmg, bmh)


def _mixer_consts(rows):
    r = jnp.arange(rows)
    tril = ((r[:, None] // CHUNK == r[None, :] // CHUNK) & (r[None, :] <= r[:, None])).astype(BF16)
    i = jnp.arange(GLA_HEADS * CHUNK) % CHUNK
    j = jnp.arange(CHUNK)
    causal = ((j[None, :] <= i[:, None]) & (j[None, :] // (CHUNK // 2) == i[:, None] // (CHUNK // 2))).astype(F32)
    bmg = (jnp.arange(GLA_V_DIM)[:, None] // GLA_HEAD_V == jnp.arange(GLA_K_DIM)[None, :] // GLA_HEAD_K).astype(F32)
    bmh = (jnp.arange(HGRN_I_DIM)[:, None] // HGRN_HEAD_I == jnp.arange(HGRN_F_DIM)[None, :] // HGRN_HEAD_F).astype(F32)
    return tril, causal, bmg, bmh


def _router_kernel(h_ref, wr_ref, br_ref, ut_ref, lt_ref, gate_ref, pos_ref, cnt_ref, off_ref, before_ref,
                   crow_ref, cexp_ref, ctot_ref, total_ref, carry_ref):
    @pl.when(pl.program_id(0) == 0)
    def _():
        carry_ref[...] = jnp.zeros_like(carry_ref)

    h = h_ref[...]
    h_hi = h.astype(BF16)
    h_lo = (h - h_hi.astype(F32)).astype(BF16)
    w = wr_ref[...]
    by_hi = _dot_nt(w, h_hi)
    l = by_hi[0:N_EXPERTS] + by_hi[N_EXPERTS:2 * N_EXPERTS] + _dot_nt(w[0:N_EXPERTS], h_lo) + br_ref[...]
    expert = lax.broadcasted_iota(I32, l.shape, 0).astype(F32)
    neg = jnp.float32(-jnp.inf)
    tops, idxs = [], []
    for _ in range(TOP_K):
        m = jnp.max(l, axis=0, keepdims=True)
        ik = jnp.min(jnp.where(l == m, expert, float(N_EXPERTS)), axis=0, keepdims=True)
        tops.append(m)
        idxs.append(ik)
        l = jnp.where(expert == ik, neg, l)
    es = [jnp.exp(t - tops[0]) for t in tops]
    denom = es[0] + es[1] + es[2] + es[3]
    onehot = jnp.zeros_like(l)
    for ik in idxs:
        onehot = onehot + jnp.where(expert == ik, 1.0, 0.0)

    carry = carry_ref[...]
    earlier = _dot(onehot.astype(BF16), ut_ref[...])
    cnt = jnp.broadcast_to(jnp.sum(onehot, axis=1, keepdims=True), carry.shape)
    phase = carry - PERM_GRAN * jnp.floor(carry * (1.0 / PERM_GRAN))
    run = jnp.where(cnt > 0.0, jnp.ceil((phase + cnt) * (1.0 / PERM_CHUNK)) * PERM_CHUNK, 0.0)
    off = _dot_f32(lt_ref[...], run)
    local = earlier + (off + phase)[:, 0:1]
    pos = [jnp.sum(jnp.where(expert == ik, local, 0.0), axis=0, keepdims=True) for ik in idxs]

    n_chunks = run * (1.0 / PERM_CHUNK)
    first_chunk = off * (1.0 / PERM_CHUNK)
    chunk = lax.broadcasted_iota(I32, carry.shape, 1).astype(F32)
    owner = lax.broadcasted_iota(I32, carry.shape, 0).astype(F32)
    owns = (chunk >= first_chunk) & (chunk < first_chunk + n_chunks)
    row0 = jnp.where(owns, (carry - phase) + PERM_CHUNK * (chunk - first_chunk), 0.0)
    list_shape = crow_ref.shape
    crow_ref[...] = jnp.broadcast_to(jnp.sum(row0, axis=0, keepdims=True), list_shape).astype(I32)
    cexp_ref[...] = jnp.broadcast_to(jnp.sum(jnp.where(owns, owner, 0.0), axis=0, keepdims=True), list_shape).astype(I32)
    ctot_ref[...] = jnp.broadcast_to(jnp.sum(n_chunks, axis=0, keepdims=True), list_shape).astype(I32)

    cnt_ref[...] = cnt.astype(I32)
    off_ref[...] = off.astype(I32)
    before_ref[...] = carry.astype(I32)
    carry = carry + cnt
    carry_ref[...] = carry
    total_ref[...] = carry.astype(I32)
    gate_ref[...] = _rows_to_sublanes([e / denom for e in es])
    pos_ref[...] = _rows_to_sublanes(pos).astype(I32)


def _router(h, w_router_p, b_router_p, layer):
    n, d = h.shape
    rows = min(PERM_ROWS, n)
    assert n % rows == 0
    nblk = n // rows
    r = jnp.arange(rows)
    upper = (r[:, None] < r[None, :]).astype(BF16)
    e = jnp.arange(N_EXPERTS)
    lower = (e[None, :] < e[:, None]).astype(F32)
    pair = pl.BlockSpec((TOP_K, rows), lambda i: (0, i))
    table = pl.BlockSpec((None, N_EXPERTS, LANES), lambda i: (i, 0, 0))
    table_shape = jax.ShapeDtypeStruct((nblk, N_EXPERTS, LANES), I32)
    assert PERM_LOCAL // PERM_CHUNK <= LANES
    chunk_list = pl.BlockSpec((None, SUBLANES, LANES), lambda i: (i, 0, 0))
    list_shape = jax.ShapeDtypeStruct((nblk, SUBLANES, LANES), I32)
    return pl.pallas_call(
        _router_kernel,
        grid=(nblk,),
        in_specs=[pl.BlockSpec((rows, d), lambda i: (i, 0)),
                  pl.BlockSpec((None, 2 * N_EXPERTS, d), lambda i: (layer, 0, 0)),
                  pl.BlockSpec((None, N_EXPERTS, 1), lambda i: (layer, 0, 0)),
                  pl.BlockSpec((rows, rows), lambda i: (0, 0)),
                  pl.BlockSpec((N_EXPERTS, N_EXPERTS), lambda i: (0, 0))],
        out_specs=[pair, pair, table, table, table, chunk_list, chunk_list, chunk_list,
                   pl.BlockSpec((N_EXPERTS, LANES), lambda i: (0, 0))],
        out_shape=[jax.ShapeDtypeStruct((TOP_K, n), F32), jax.ShapeDtypeStruct((TOP_K, n), I32),
                   table_shape, table_shape, table_shape, list_shape, list_shape, list_shape,
                   jax.ShapeDtypeStruct((N_EXPERTS, LANES), I32)],
        scratch_shapes=[pltpu.VMEM((N_EXPERTS, LANES), F32)],
        compiler_params=pltpu.CompilerParams(dimension_semantics=("arbitrary",), vmem_limit_bytes=VMEM_LIMIT_BYTES),
        name=f"router_l{layer}",
    )(h, w_router_p, b_router_p, upper, lower)


def _for_each_chunk(n_chunks, crow_ref, table_row, chunk_fn):
    base = table_row * LANES

    def body(j, carry):
        chunk_fn(pl.multiple_of(j * PERM_CHUNK, PERM_CHUNK), pl.multiple_of(crow_ref[base + j], PERM_GRAN))
        return carry

    lax.fori_loop(0, n_chunks, body, 0)


def _dispatch_kernel(pend_ref, pcnt_ref, cnt_ref, off_ref, before_ref, crow_ref, ctot_ref, h_ref, pos_ref, xs_ref,
                     xc_ref, zbuf_ref, tail_ref, tot_ref, sem, zsem, *, block_rows):
    i = pl.program_id(0)
    slot = i % 2

    def fill(first_row):
        start = pl.multiple_of(first_row, block_rows)
        return pltpu.make_async_copy(zbuf_ref, xs_ref.at[pl.ds(start, block_rows)], zsem)

    @pl.when(i == 0)
    def _():
        zbuf_ref[...] = jnp.zeros_like(zbuf_ref)
        tail_ref[...] = jnp.zeros_like(tail_ref)
        used_blocks = pend_ref[N_EXPERTS - 1] // block_rows
        all_blocks = xs_ref.shape[0] // block_rows
        for e in range(N_EXPERTS):
            @pl.when(pcnt_ref[e] > 0)
            def _():
                fill(pend_ref[e] - block_rows).start()
        lax.fori_loop(used_blocks, all_blocks, lambda j, c: (fill(j * block_rows).start(), c)[1], 0)
        for e in range(N_EXPERTS):
            @pl.when(pcnt_ref[e] > 0)
            def _():
                fill(pend_ref[e] - block_rows).wait()
        lax.fori_loop(used_blocks, all_blocks, lambda j, c: (fill(j * block_rows).wait(), c)[1], 0)

    pos = pos_ref[...]
    local_row = lax.broadcasted_iota(I32, (PERM_LOCAL, pos.shape[1]), 0)
    p = jnp.where(local_row == pos[0:1], 1.0, 0.0)
    for k in range(1, TOP_K):
        p = p + jnp.where(local_row == pos[k:k + 1], 1.0, 0.0)
    pb = p.astype(BF16)
    hb = h_ref[...].astype(BF16)
    half = hb.shape[1] // 2
    xc_ref[slot] = _pack_bf16_pair(_dot(pb, hb[:, :half]), _dot(pb, hb[:, half:]))

    def copy(s, local_row, global_row, size):
        return pltpu.make_async_copy(xc_ref.at[s, pl.ds(local_row, size)],
                                     xs_ref.at[pl.ds(global_row, size)], sem.at[s])

    def drain(s):
        @pl.when(tot_ref[0] > 0)
        def _():
            copy(s, 0, 0, tot_ref[0] * PERM_CHUNK).wait()

    @pl.when(i > 0)
    def _():
        drain(1 - slot)

    group_row = lax.broadcasted_iota(I32, (PERM_GRAN, 1), 0)

    def carry_partial_group(e, n, loc, phase):
        first = pl.ds(pl.multiple_of(loc, PERM_GRAN), PERM_GRAN)
        merged = jnp.where((group_row < phase) & (n > 0), tail_ref[e], xc_ref[slot, first, :])
        xc_ref[slot, first, :] = merged
        end = pl.ds(pl.multiple_of(loc + ((phase + n) & -PERM_GRAN), PERM_GRAN), PERM_GRAN)
        tail_ref[e] = jnp.where(n > 0, xc_ref[slot, end, :], tail_ref[e])

    base = (i % TABLE_GROUP) * LANES
    for e in range(N_EXPERTS):
        carry_partial_group(e, cnt_ref[base + e], off_ref[base + e], before_ref[base + e] & (PERM_GRAN - 1))
    tot_ref[0] = ctot_ref[i % SMEM_TABLE]
    _for_each_chunk(tot_ref[0], crow_ref, i % TABLE_GROUP, lambda lr, gr: copy(slot, lr, gr, PERM_CHUNK).start())

    @pl.when(i == pl.num_programs(0) - 1)
    def _():
        drain(slot)


def _table_spec(shift, nblk, per_block=LANES):
    group = SMEM_TABLE // per_block
    return pl.BlockSpec((SMEM_TABLE,), lambda i, *_: (jnp.minimum(i + shift, nblk - 1) // group,),
                        memory_space=pltpu.SMEM)


def _dispatch(h, pos, cnt_tbl, off_tbl, before_tbl, crow_tbl, ctot_tbl, pend, pcnt, total_rows, layer):
    n, d = h.shape
    rows = min(PERM_ROWS, n)
    nblk = n // rows
    return pl.pallas_call(
        functools.partial(_dispatch_kernel, block_rows=EXPERT_ROWS),
        grid_spec=pltpu.PrefetchScalarGridSpec(
            num_scalar_prefetch=2,
            grid=(nblk,),
            in_specs=[_table_spec(0, nblk), _table_spec(0, nblk), _table_spec(0, nblk), _table_spec(0, nblk),
                      _table_spec(0, nblk, per_block=1),
                      pl.BlockSpec((rows, d), lambda i, *_: (i, 0)),
                      pl.BlockSpec((TOP_K, rows), lambda i, *_: (0, i))],
            out_specs=pl.BlockSpec(memory_space=pl.ANY),
            scratch_shapes=[pltpu.VMEM((2, PERM_LOCAL, d // 2), U32), pltpu.VMEM((EXPERT_ROWS, d // 2), U32),
                            pltpu.VMEM((N_EXPERTS, PERM_GRAN, d // 2), U32), pltpu.SMEM((1,), I32),
                            pltpu.SemaphoreType.DMA((2,)), pltpu.SemaphoreType.DMA(())],
        ),
        out_shape=jax.ShapeDtypeStruct((total_rows, d // 2), U32),
        compiler_params=pltpu.CompilerParams(dimension_semantics=("arbitrary",), vmem_limit_bytes=VMEM_LIMIT_BYTES),
        name=f"dispatch_l{layer}",
    )(pend, pcnt, cnt_tbl, off_tbl, before_tbl, crow_tbl, ctot_tbl, h, pos)


def _combine_kernel(crow_ref, ctot_ref, ncrow_ref, nctot_ref, h_ref, gate_ref, pos_ref,
                    lg_ref, lb_ref, ys_ref, o_ref, yc_ref, sem, *, alpha):
    i = pl.program_id(0)
    slot = i % 2

    def copy(s, local_row, global_row, size):
        return pltpu.make_async_copy(ys_ref.at[pl.ds(global_row, size)],
                                     yc_ref.at[s, pl.ds(local_row, size)], sem.at[s])

    @pl.when(i == 0)
    def _():
        yc_ref[...] = jnp.zeros_like(yc_ref)
        _for_each_chunk(ctot_ref[0], crow_ref, 0, lambda lr, gr: copy(0, lr, gr, PERM_CHUNK).start())

    @pl.when(i + 1 < pl.num_programs(0))
    def _():
        _for_each_chunk(nctot_ref[(i + 1) % SMEM_TABLE], ncrow_ref, (i + 1) % TABLE_GROUP,
                        lambda lr, gr: copy(1 - slot, lr, gr, PERM_CHUNK).start())

    n_chunks = ctot_ref[i % SMEM_TABLE]

    @pl.when(n_chunks > 0)
    def _():
        copy(slot, 0, 0, n_chunks * PERM_CHUNK).wait()

    pos = pos_ref[...]
    gates = gate_ref[...]
    local_row = lax.broadcasted_iota(I32, (PERM_LOCAL, pos.shape[1]), 0)
    q = jnp.where(local_row == pos[0:1], gates[0:1], 0.0)
    for k in range(1, TOP_K):
        q = q + jnp.where(local_row == pos[k:k + 1], gates[k:k + 1], 0.0)
    qb = q.astype(BF16)
    y_lo, y_hi = _unpack_bf16_pair(yc_ref[slot])
    ffn = jnp.concatenate([_dot_tn(qb, y_lo), _dot_tn(qb, y_hi)], axis=-1)
    o_ref[...] = _layer_norm(alpha * h_ref[...] + ffn, lg_ref[...], lb_ref[...])


def _combine(h, gates, pos, crow_tbl, ctot_tbl, ys, ln_g, ln_b, layer, alpha):
    n, d = h.shape
    rows = min(PERM_ROWS, n)
    nblk = n // rows
    return pl.pallas_call(
        functools.partial(_combine_kernel, alpha=alpha),
        grid=(nblk,),
        in_specs=[_table_spec(0, nblk), _table_spec(0, nblk, per_block=1),
                  _table_spec(1, nblk), _table_spec(1, nblk, per_block=1),
                  pl.BlockSpec((rows, d), lambda i: (i, 0)),
                  pl.BlockSpec((TOP_K, rows), lambda i: (0, i)),
                  pl.BlockSpec((TOP_K, rows), lambda i: (0, i)),
                  pl.BlockSpec((None, 1, d), lambda i: (layer, 0, 0)),
                  pl.BlockSpec((None, 1, d), lambda i: (layer, 0, 0)),
                  pl.BlockSpec(memory_space=pl.ANY)],
        out_specs=pl.BlockSpec((rows, d), lambda i: (i, 0)),
        out_shape=jax.ShapeDtypeStruct((n, d), F32),
        scratch_shapes=[pltpu.VMEM((2, PERM_LOCAL, d // 2), U32), pltpu.SemaphoreType.DMA((2,))],
        compiler_params=pltpu.CompilerParams(dimension_semantics=("arbitrary",), vmem_limit_bytes=VMEM_LIMIT_BYTES),
        name=f"combine_l{layer}",
    )(crow_tbl, ctot_tbl, crow_tbl, ctot_tbl, h, gates, pos, ln_g, ln_b, ys)


def _expert_kernel(be_ref, nact_ref, x_ref, wgu_ref, bgu_ref, wd_ref, bd_ref, y_ref, wgu_b_ref, wd_b_ref):
    j = pl.program_id(0)
    d_ff = wd_ref.shape[0]
    half = y_ref.shape[1]

    @pl.when((j == 0) | (be_ref[j] != be_ref[jnp.maximum(j - 1, 0)]))
    def _():
        wgu_b_ref[...] = wgu_ref[...].astype(BF16)
        wd_b_ref[...] = wd_ref[...].astype(BF16)

    @pl.when(j < nact_ref[0])
    def _():
        x = jnp.concatenate(_unpack_bf16_pair(x_ref[...]), axis=-1)
        gu = _dot(x, wgu_b_ref[...]) + bgu_ref[...]
        gate = jnp.minimum(gu[:, :d_ff], SWIGLU_LIMIT)
        up = jnp.clip(gu[:, d_ff:], -SWIGLU_LIMIT, SWIGLU_LIMIT)
        act = (up + 1.0) * gate * _sigmoid(SWIGLU_ALPHA * gate)
        y = _dot(act.astype(BF16), wd_b_ref[...]) + bd_ref[...]
        y_ref[...] = _pack_bf16_pair(_round_bf16(y[:, :half]), _round_bf16(y[:, half:]))

    @pl.when(j >= nact_ref[0])
    def _():
        y_ref[...] = jnp.zeros_like(y_ref)


def _experts(xs, blk_expert, nact, wgu, bgu, wd, bd, layer):
    total_rows, half = xs.shape
    nblk = total_rows // EXPERT_ROWS
    d = wgu.shape[2]
    d_ff = wd.shape[2]

    def row_block(j, be, na):
        return (j, 0)

    def of_expert(j, be, na):
        return (layer, be[j], 0, 0)

    return pl.pallas_call(
        _expert_kernel,
        grid_spec=pltpu.PrefetchScalarGridSpec(
            num_scalar_prefetch=2,
            grid=(nblk,),
            in_specs=[pl.BlockSpec((EXPERT_ROWS, half), row_block),
                      pl.BlockSpec((None, None, d, 2 * d_ff), of_expert),
                      pl.BlockSpec((None, None, 1, 2 * d_ff), of_expert),
                      pl.BlockSpec((None, None, d_ff, d), of_expert),
                      pl.BlockSpec((None, None, 1, d), of_expert)],
            out_specs=pl.BlockSpec((EXPERT_ROWS, half), row_block),
            scratch_shapes=[pltpu.VMEM((d, 2 * d_ff), BF16), pltpu.VMEM((d_ff, d), BF16)],
        ),
        out_shape=jax.ShapeDtypeStruct((total_rows, half), U32),
        compiler_params=pltpu.CompilerParams(dimension_semantics=("arbitrary",), vmem_limit_bytes=VMEM_LIMIT_BYTES),
        name=f"experts_l{layer}",
    )(blk_expert, nact, xs, wgu, bgu, wd, bd)


def _moe(h, w_router_p, b_router_p, wgu, bgu, wd, bd, ln_g, ln_b, layer, alpha):
    n, d = h.shape
    (gates, pos, cnt_tbl, off_tbl, before_tbl, crow_tbl, cexp_tbl, ctot_tbl,
     totals) = _router(h, w_router_p, b_router_p, layer)
    counts = totals[:, 0]
    padded = (counts + PERM_CHUNK + EXPERT_ROWS - 1) // EXPERT_ROWS * EXPERT_ROWS
    padded = jnp.where(counts > 0, padded, 0)
    pend = jnp.cumsum(padded).astype(I32)
    pstart = pend - padded
    nblk = (n * TOP_K + N_EXPERTS * PERM_CHUNK) // EXPERT_ROWS + N_EXPERTS
    nact = (pend[-1] // EXPERT_ROWS).reshape(1).astype(I32)
    blk = jnp.minimum(jnp.arange(nblk, dtype=I32), nact[0] - 1) * EXPERT_ROWS
    blk_expert = jnp.minimum(jnp.sum(pend[None, :] <= blk[:, None], axis=1), N_EXPERTS - 1).astype(I32)

    def flat(tbl):
        tbl = tbl.reshape(-1)
        return jnp.pad(tbl, (0, -tbl.shape[0] % SMEM_TABLE))

    def per_expert(tbl):
        return flat(jnp.pad(tbl[:, :, 0], ((0, 0), (0, LANES - N_EXPERTS))))

    cexp = cexp_tbl[:, 0, :]
    region = jnp.sum(jnp.where(cexp[:, :, None] == jnp.arange(N_EXPERTS)[None, None, :], pstart[None, None, :], 0), axis=-1)
    crow_f = flat(crow_tbl[:, 0, :] + region)
    ctot_f = flat(ctot_tbl[:, 0, 0])

    xs = _dispatch(h, pos, per_expert(cnt_tbl), per_expert(off_tbl), per_expert(before_tbl), crow_f, ctot_f,
                   pend, padded.astype(I32), nblk * EXPERT_ROWS, layer)
    ys = _experts(xs, blk_expert, nact, wgu, bgu, wd, bd, layer)
    return _combine(h, gates, pos, crow_f, ctot_f, ys, ln_g, ln_b, layer, alpha)


def _relayout_w_in(w_in):
    depth, d, _ = w_in.shape
    out = jnp.zeros((depth, d, D_IN_PAD), BF16)
    for g in range(len(IN_SPLITS)):
        lo, hi = _COLS[g]
        out = out.at[:, :, lo:hi].set(w_in[:, :, _SRC_OFF[g]:_SRC_OFF[g] + IN_SPLITS[g]].astype(BF16))
    return out


def kernel(x, ln_in_g, ln_in_b, w_in, w_gla_gate_expand, b_gla_gate, gla_norm_g, hgrn_lb_raw, hgrn_norm_g, conv_w, w_out, ln_mix_g, ln_mix_b, w_router, b_router, w_gate_up, b_gate_up, w_down, b_down, ln_ffn_g, ln_ffn_b):
    bsz, t, d = x.shape
    depth = w_in.shape[0]
    alpha = (2 * depth) ** 0.25
    n = bsz * t

    w_in_r = _relayout_w_in(w_in)
    w_exp = jnp.zeros((depth, LANES, GLA_K_DIM), F32).at[:, :GLA_GATE_RANK, :].set(w_gla_gate_expand)
    w_exp_hi = w_exp.astype(BF16)
    w_exp = jnp.concatenate([w_exp_hi, (w_exp - w_exp_hi.astype(F32)).astype(BF16)], axis=-1)
    b_gate = b_gla_gate.reshape(depth, 1, GLA_K_DIM)
    gla_g = jnp.tile(gla_norm_g, (1, GLA_HEADS)).reshape(depth, 1, GLA_V_DIM)
    hgrn_g = jnp.tile(hgrn_norm_g, (1, HGRN_HEADS)).reshape(depth, 1, HGRN_I_DIM)
    w_out_b = w_out.astype(BF16)
    w_router_t = jnp.swapaxes(w_router, 1, 2)
    w_router_hi = w_router_t.astype(BF16)
    w_router_lo = (w_router_t - w_router_hi.astype(F32)).astype(BF16)
    w_router_p = jnp.concatenate([w_router_hi, w_router_lo], axis=1)
    b_router_p = b_router.reshape(depth, N_EXPERTS, 1)
    wgu, wd = w_gate_up, w_down
    bgu = b_gate_up.reshape(depth, N_EXPERTS, 1, -1)
    bd = b_down.reshape(depth, N_EXPERTS, 1, -1)
    row = lambda a: a.reshape(depth, 1, d)
    consts = _mixer_consts(min(CUMSUM_ROWS, t))

    h = x
    for l in range(depth):
        h = _mixer(h, ln_in_g.reshape(1, d), ln_in_b.reshape(1, d), w_in_r, w_exp, b_gate, gla_g, hgrn_lb_raw,
                   hgrn_g, conv_w, w_out_b, row(ln_mix_g), row(ln_mix_b), consts,
                   layer=l, pre_ln=(l == 0), alpha=alpha)
        h = _moe(h.reshape(n, d), w_router_p, b_router_p, wgu, bgu, wd, bd, row(ln_ffn_g), row(ln_ffn_b),
                 l, alpha).reshape(bsz, t, d)
    return h
```

```python
import functools

import jax
import jax.numpy as jnp
from jax import lax
from jax.experimental import pallas as pl
from jax.experimental.pallas import tpu as pltpu

F32 = jnp.float32
BF16 = jnp.bfloat16
I32 = jnp.int32

GLA_HEADS = 4
GLA_HEAD_K = 64
GLA_HEAD_V = 128
GLA_GATE_RANK = 16
GLA_GATE_NORMALIZER = 16.0
HGRN_HEADS = 4
HGRN_HEAD_F = 64
HGRN_HEAD_I = 64
CONV_CHANNELS = 256
CONV_WIDTH = 3
CHUNK = 64
N_EXPERTS = 32
TOP_K = 4
SWIGLU_LIMIT = 7.0
SWIGLU_ALPHA = 1.702
LN_EPS = 1e-5
RMS_EPS = 1e-6

GLA_K_DIM = GLA_HEADS * GLA_HEAD_K
GLA_V_DIM = GLA_HEADS * GLA_HEAD_V
HGRN_F_DIM = HGRN_HEADS * HGRN_HEAD_F
HGRN_I_DIM = HGRN_HEADS * HGRN_HEAD_I
IN_SPLITS = (GLA_K_DIM, GLA_K_DIM, GLA_V_DIM, GLA_GATE_RANK, GLA_V_DIM,
             HGRN_F_DIM, HGRN_F_DIM, HGRN_I_DIM, HGRN_I_DIM,
             CONV_CHANNELS, CONV_CHANNELS, CONV_CHANNELS)

LANES = 128
SUBLANES = 8
BF16_SUBLANES = 16
VMEM_LIMIT_BYTES = 56 * 1024 * 1024

MIX_ROWS = 1024
CUMSUM_ROWS = 256
PERM_ROWS = 256
PERM_GRAN = BF16_SUBLANES
PERM_CHUNK = 16
PERM_LOCAL = -(-(PERM_ROWS * TOP_K + N_EXPERTS * (PERM_GRAN + PERM_CHUNK - 2)) // LANES) * LANES
PERM_LOCAL_SIZES = (PERM_LOCAL - 4 * LANES, PERM_LOCAL - 2 * LANES, PERM_LOCAL)
EXPERT_ROWS = 512
SMEM_TABLE = 1024
TABLE_GROUP = SMEM_TABLE // LANES

_ORDER = (0, 1, 2, 4, 5, 6, 7, 8, 9, 10, 11, 3)
_SRC_OFF = [sum(IN_SPLITS[:i]) for i in range(len(IN_SPLITS))]
_COLS = {}
_off = 0
for _g in _ORDER:
    _w = IN_SPLITS[_g]
    _COLS[_g] = (_off, _off + _w)
    _off += -(-_w // LANES) * LANES
D_IN_PAD = _off
(C_GQ, C_GK, C_GV, C_GLR, C_GOG, C_HQ, C_HF, C_HI, C_HOG, C_CB, C_CC, C_CH) = (_COLS[i] for i in range(12))


def _layer_norm(x, g, b):
    mu = jnp.mean(x, axis=-1, keepdims=True)
    xc = x - mu
    var = jnp.mean(xc * xc, axis=-1, keepdims=True)
    return xc * lax.rsqrt(var + LN_EPS) * g + b


def _sigmoid(x):
    return 1.0 / (1.0 + jnp.exp(-x))


def _dot(a, b):
    return jnp.dot(a, b, preferred_element_type=F32)


def _dot_nt(a, b):
    return lax.dot_general(a, b, (((1,), (1,)), ((), ())), preferred_element_type=F32)


def _dot_tn(a, b):
    return lax.dot_general(a, b, (((0,), (0,)), ((), ())), preferred_element_type=F32)


def _dot_f32(a, b):
    return lax.dot_general(a, b, (((1,), (0,)), ((), ())), precision=lax.Precision.HIGHEST,
                           preferred_element_type=F32)


def _split_dot(a01, x):
    hi = x.astype(BF16)
    lo = (x - hi.astype(F32)).astype(BF16)
    return _dot(a01, hi) + _dot(a01, lo)


def _rows_to_sublanes(rows):
    sub = lax.broadcasted_iota(I32, (len(rows), rows[0].shape[1]), 0)
    out = jnp.broadcast_to(rows[-1], sub.shape)
    for k in range(len(rows) - 2, -1, -1):
        out = jnp.where(sub == k, rows[k], out)
    return out


def _chunk_scores(q, k, gc, lane_k, heads, causal):
    half = CHUNK // 2
    row = lax.broadcasted_iota(I32, (CHUNK, 1), 0)
    second = row >= half
    mid = jnp.where(second, gc[half + half // 2 - 1:half + half // 2], gc[half // 2 - 1:half // 2])
    q_same = q * jnp.exp(gc - mid)
    k_same = k * jnp.exp(mid - gc)
    edge = gc[half - 1:half]
    q_cross = q[half:] * jnp.exp(gc[half:] - edge)
    k_cross = jnp.where(second, 0.0, k * jnp.exp(edge - gc))

    def stack(x):
        return jnp.concatenate([jnp.where(lane_k == h, x, 0.0) for h in range(heads)], axis=0).astype(BF16)

    same = _dot_nt(stack(q_same), k_same.astype(BF16))
    cross = _dot_nt(stack(q_cross), k_cross.astype(BF16))
    none = jnp.zeros((half, CHUNK), F32)
    cross = jnp.concatenate([blk for h in range(heads) for blk in (none, cross[h * half:(h + 1) * half])], axis=0)
    return jnp.where(causal, same, 0.0) + cross


def _chunk_recurrence(q, k, v, g, st_ref, tril, causal, bmask, heads, dk, dv):
    rows = q.shape[0]
    cum = tril.shape[0]
    gc = jnp.concatenate([_split_dot(tril, g[r:r + cum]) for r in range(0, rows, cum)], axis=0)
    lane_k = lax.broadcasted_iota(I32, (1, heads * dk), 1) // dk
    lane_v = lax.broadcasted_iota(I32, (1, heads * dv), 1) // dv
    st = st_ref[...]
    outs = []
    for c in range(rows // CHUNK):
        sl = slice(c * CHUNK, (c + 1) * CHUNK)
        gcc, qc, kc, vc = gc[sl], q[sl], k[sl], v[sl]
        last = gcc[CHUNK - 1:CHUNK]
        qi = qc * jnp.exp(gcc)
        ks = kc * jnp.exp(last - gcc)
        vb = vc.astype(BF16)
        ab = _chunk_scores(qc, kc, gcc, lane_k, heads, causal).astype(BF16)
        if dv % LANES == 0:
            o_intra = jnp.concatenate([_dot(ab[h * CHUNK:(h + 1) * CHUNK], vb[:, h * dv:(h + 1) * dv])
                                       for h in range(heads)], axis=-1)
        else:
            o_all = _dot(ab, vb)
            o_intra = jnp.where(lane_v == 0, o_all[0:CHUNK], 0.0)
            for h in range(1, heads):
                o_intra = o_intra + jnp.where(lane_v == h, o_all[h * CHUNK:(h + 1) * CHUNK], 0.0)
        o_inter = _dot_nt(qi.astype(BF16), st.astype(BF16))
        u = _dot_tn(vb, ks.astype(BF16))
        st = jnp.exp(last) * st + u * bmask
        outs.append(o_intra + o_inter)
    st_ref[...] = st
    return jnp.concatenate(outs, axis=0)


def _gated_head_rmsnorm(o, gate, g_tiled, heads, dv):
    lane_v = lax.broadcasted_iota(I32, (1, heads * dv), 1) // dv
    sq = o * o
    ms = jnp.zeros_like(o)
    for h in range(heads):
        m = lane_v == h
        ms = jnp.where(m, jnp.sum(jnp.where(m, sq, 0.0), axis=-1, keepdims=True), ms)
    ms = ms * (1.0 / dv)
    return o * lax.rsqrt(ms + RMS_EPS) * g_tiled * (gate * _sigmoid(gate))


def _mixer_kernel(x_ref, lng_ref, lnb_ref, win_ref, wexp_ref, bgate_ref, glag_ref, lbraw_ref, hgg_ref,
                  convw_ref, wout_ref, mg_ref, mb_ref, tril_ref, causal_ref, bmg_ref, bmh_ref,
                  o_ref, sg_ref, sh_ref, cc_ref, *, layer, pre_ln, alpha):
    @pl.when(pl.program_id(1) == 0)
    def _():
        sg_ref[...] = jnp.zeros_like(sg_ref)
        sh_ref[...] = jnp.zeros_like(sh_ref)
        cc_ref[...] = jnp.zeros_like(cc_ref)

    x = x_ref[...]
    h = _layer_norm(x, lng_ref[...], lnb_ref[...]) if pre_ln else x
    hb = h.astype(BF16)
    rows = x.shape[0]

    def proj(cols):
        return _dot(hb, win_ref[:, cols[0]:cols[0] + -(-(cols[1] - cols[0]) // LANES) * LANES])

    tril = tril_ref[...]
    causal = causal_ref[...] > 0.5

    q = proj(C_GQ) * (GLA_HEAD_K ** -0.5)
    lr = proj(C_GLR)
    lr_hi = lr.astype(BF16)
    lr_lo = (lr - lr_hi.astype(F32)).astype(BF16)
    by_hi = _dot(lr_hi, wexp_ref[...])
    gate_logits = (by_hi[:, :GLA_K_DIM] + by_hi[:, GLA_K_DIM:] + _dot(lr_lo, wexp_ref[:, 0:GLA_K_DIM])
                   + bgate_ref[...])
    log_a = (jnp.minimum(gate_logits, 0.0) - jnp.log(1.0 + jnp.exp(-jnp.abs(gate_logits)))) * (1.0 / GLA_GATE_NORMALIZER)
    o_gla = _chunk_recurrence(q, proj(C_GK), proj(C_GV), log_a, sg_ref, tril, causal, bmg_ref[...],
                              GLA_HEADS, GLA_HEAD_K, GLA_HEAD_V)
    y_gla = _gated_head_rmsnorm(o_gla, proj(C_GOG), glag_ref[...], GLA_HEADS, GLA_HEAD_V)

    raw = lbraw_ref[...]
    e = jnp.exp(raw - jnp.max(raw, axis=0, keepdims=True))
    lb = jnp.zeros_like(e[0:1])
    for i in range(1, layer + 1):
        lb = lb + e[i:i + 1]
    lb = lb / jnp.sum(e, axis=0, keepdims=True)
    f = lb + (1.0 - lb) * _sigmoid(proj(C_HF))
    hq = proj(C_HQ)
    o_h = _chunk_recurrence(hq * _sigmoid(hq), 1.0 - f, proj(C_HI), jnp.log(f), sh_ref, tril, causal,
                            bmh_ref[...], HGRN_HEADS, HGRN_HEAD_F, HGRN_HEAD_I)
    y_h = _gated_head_rmsnorm(o_h, proj(C_HOG), hgg_ref[...], HGRN_HEADS, HGRN_HEAD_I)

    u = proj(C_CC) * proj(C_CH)
    prev = cc_ref[...]
    row = lax.broadcasted_iota(I32, (rows, 1), 0)
    u1 = jnp.where(row == 0, prev[SUBLANES - 1:SUBLANES], pltpu.roll(u, 1, 0))
    u2 = jnp.where(row == 0, prev[SUBLANES - 2:SUBLANES - 1],
                   jnp.where(row == 1, prev[SUBLANES - 1:SUBLANES], pltpu.roll(u, 2, 0)))
    cc_ref[...] = u[rows - SUBLANES:rows]
    cw = convw_ref[...]
    y_conv = proj(C_CB) * (cw[0:1] * u2 + cw[1:2] * u1 + cw[2:3] * u)

    o1, o2 = GLA_V_DIM, GLA_V_DIM + HGRN_I_DIM
    mix = (_dot(y_gla.astype(BF16), wout_ref[0:o1, :]) + _dot(y_h.astype(BF16), wout_ref[o1:o2, :])
           + _dot(y_conv.astype(BF16), wout_ref[o2:o2 + CONV_CHANNELS, :]))
    o_ref[...] = _layer_norm(alpha * h + mix, mg_ref[...], mb_ref[...])


def _mixer(x, ln_g, ln_b, w_in_r, w_exp, b_gate, gla_g, lb_raw, hgrn_g, conv_w, w_out, mg, mb,
           consts, *, layer, pre_ln, alpha):
    bsz, t, d = x.shape
    rows = min(MIX_ROWS, t)
    tril, causal, bmg, bmh = consts
    assert t % rows == 0 and rows % tril.shape[0] == 0 and tril.shape[0] % CHUNK == 0

    def lay(shape):
        return pl.BlockSpec((None,) + shape, lambda b, i: (layer,) + (0,) * len(shape))

    def whole(a):
        return pl.BlockSpec(a.shape, lambda b, i: (0,) * a.ndim)

    return pl.pallas_call(
        functools.partial(_mixer_kernel, layer=layer, pre_ln=pre_ln, alpha=alpha),
        grid=(bsz, t // rows),
        in_specs=[
            pl.BlockSpec((None, rows, d), lambda b, i: (b, i, 0)),
            whole(ln_g), whole(ln_b),
            lay((d, D_IN_PAD)), lay((LANES, 2 * GLA_K_DIM)), lay((1, GLA_K_DIM)), lay((1, GLA_V_DIM)),
            whole(lb_raw), lay((1, HGRN_I_DIM)), lay((CONV_WIDTH, CONV_CHANNELS)), lay((d, d)),
            lay((1, d)), lay((1, d)),
            whole(tril), whole(causal), whole(bmg), whole(bmh),
        ],
        out_specs=pl.BlockSpec((None, rows, d), lambda b, i: (b, i, 0)),
        out_shape=jax.ShapeDtypeStruct((bsz, t, d), F32),
        scratch_shapes=[pltpu.VMEM((GLA_V_DIM, GLA_K_DIM), F32), pltpu.VMEM((HGRN_I_DIM, HGRN_F_DIM), F32),
                        pltpu.VMEM((SUBLANES, CONV_CHANNELS), F32)],
        compiler_params=pltpu.CompilerParams(dimension_semantics=("arbitrary", "arbitrary"),
                                             vmem_limit_bytes=VMEM_LIMIT_BYTES),
        name=f"mixer_l{layer}",
    )(x, ln_g, ln_b, w_in_r, w_exp, b_gate, gla_g, lb_raw, hgrn_g, conv_w, w_out, mg, mb, tril, causal, bmg, bmh)


def _mixer_consts(rows):
    r = jnp.arange(rows)
    tril = ((r[:, None] // CHUNK == r[None, :] // CHUNK) & (r[None, :] <= r[:, None])).astype(BF16)
    i = jnp.arange(GLA_HEADS * CHUNK) % CHUNK
    j = jnp.arange(CHUNK)
    causal = ((j[None, :] <= i[:, None]) & (j[None, :] // (CHUNK // 2) == i[:, None] // (CHUNK // 2))).astype(F32)
    bmg = (jnp.arange(GLA_V_DIM)[:, None] // GLA_HEAD_V == jnp.arange(GLA_K_DIM)[None, :] // GLA_HEAD_K).astype(F32)
    bmh = (jnp.arange(HGRN_I_DIM)[:, None] // HGRN_HEAD_I == jnp.arange(HGRN_F_DIM)[None, :] // HGRN_HEAD_F).astype(F32)
    return tril, causal, bmg, bmh


def _router_kernel(h_ref, wr_ref, br_ref, ut_ref, lt_ref, gate_ref, pos_ref, cnt_ref, off_ref, before_ref,
                   crow_ref, cexp_ref, ctot_ref, total_ref, carry_ref):
    @pl.when(pl.program_id(0) == 0)
    def _():
        carry_ref[...] = jnp.zeros_like(carry_ref)

    h = h_ref[...]
    h_hi = h.astype(BF16)
    h_lo = (h - h_hi.astype(F32)).astype(BF16)
    w = wr_ref[...]
    by_hi = _dot_nt(w, h_hi)
    l = by_hi[0:N_EXPERTS] + by_hi[N_EXPERTS:2 * N_EXPERTS] + _dot_nt(w[0:N_EXPERTS], h_lo) + br_ref[...]
    expert = lax.broadcasted_iota(I32, l.shape, 0).astype(F32)
    neg = jnp.float32(-jnp.inf)
    tops, idxs = [], []
    for _ in range(TOP_K):
        m = jnp.max(l, axis=0, keepdims=True)
        ik = jnp.min(jnp.where(l == m, expert, float(N_EXPERTS)), axis=0, keepdims=True)
        tops.append(m)
        idxs.append(ik)
        l = jnp.where(expert == ik, neg, l)
    es = [jnp.exp(t - tops[0]) for t in tops]
    denom = es[0] + es[1] + es[2] + es[3]
    onehot = jnp.zeros_like(l)
    for ik in idxs:
        onehot = onehot + jnp.where(expert == ik, 1.0, 0.0)

    carry = carry_ref[...]
    earlier = _dot(onehot.astype(BF16), ut_ref[...])
    cnt = jnp.broadcast_to(jnp.sum(onehot, axis=1, keepdims=True), carry.shape)
    phase = carry - PERM_GRAN * jnp.floor(carry * (1.0 / PERM_GRAN))
    run = jnp.where(cnt > 0.0, jnp.ceil((phase + cnt) * (1.0 / PERM_CHUNK)) * PERM_CHUNK, 0.0)
    off = _dot_f32(lt_ref[...], run)
    local = earlier + (off + phase)[:, 0:1]
    pos = [jnp.sum(jnp.where(expert == ik, local, 0.0), axis=0, keepdims=True) for ik in idxs]

    n_chunks = run * (1.0 / PERM_CHUNK)
    first_chunk = off * (1.0 / PERM_CHUNK)
    chunk = lax.broadcasted_iota(I32, carry.shape, 1).astype(F32)
    owner = lax.broadcasted_iota(I32, carry.shape, 0).astype(F32)
    owns = (chunk >= first_chunk) & (chunk < first_chunk + n_chunks)
    row0 = jnp.where(owns, (carry - phase) + PERM_CHUNK * (chunk - first_chunk), 0.0)
    list_shape = crow_ref.shape
    crow_ref[...] = jnp.broadcast_to(jnp.sum(row0, axis=0, keepdims=True), list_shape).astype(I32)
    cexp_ref[...] = jnp.broadcast_to(jnp.sum(jnp.where(owns, owner, 0.0), axis=0, keepdims=True), list_shape).astype(I32)
    ctot_ref[...] = jnp.broadcast_to(jnp.sum(n_chunks, axis=0, keepdims=True), list_shape).astype(I32)

    cnt_ref[...] = cnt.astype(I32)
    off_ref[...] = off.astype(I32)
    before_ref[...] = carry.astype(I32)
    carry = carry + cnt
    carry_ref[...] = carry
    total_ref[...] = carry.astype(I32)
    gate_ref[...] = _rows_to_sublanes([e / denom for e in es])
    pos_ref[...] = _rows_to_sublanes(pos).astype(I32)


def _router(h, w_router_p, b_router_p, layer):
    n, d = h.shape
    rows = min(PERM_ROWS, n)
    assert n % rows == 0
    nblk = n // rows
    r = jnp.arange(rows)
    upper = (r[:, None] < r[None, :]).astype(BF16)
    e = jnp.arange(N_EXPERTS)
    lower = (e[None, :] < e[:, None]).astype(F32)
    pair = pl.BlockSpec((TOP_K, rows), lambda i: (0, i))
    table = pl.BlockSpec((None, N_EXPERTS, LANES), lambda i: (i, 0, 0))
    table_shape = jax.ShapeDtypeStruct((nblk, N_EXPERTS, LANES), I32)
    assert PERM_LOCAL // PERM_CHUNK <= LANES
    chunk_list = pl.BlockSpec((None, SUBLANES, LANES), lambda i: (i, 0, 0))
    list_shape = jax.ShapeDtypeStruct((nblk, SUBLANES, LANES), I32)
    return pl.pallas_call(
        _router_kernel,
        grid=(nblk,),
        in_specs=[pl.BlockSpec((rows, d), lambda i: (i, 0)),
                  pl.BlockSpec((None, 2 * N_EXPERTS, d), lambda i: (layer, 0, 0)),
                  pl.BlockSpec((None, N_EXPERTS, 1), lambda i: (layer, 0, 0)),
                  pl.BlockSpec((rows, rows), lambda i: (0, 0)),
                  pl.BlockSpec((N_EXPERTS, N_EXPERTS), lambda i: (0, 0))],
        out_specs=[pair, pair, table, table, table, chunk_list, chunk_list, chunk_list,
                   pl.BlockSpec((N_EXPERTS, LANES), lambda i: (0, 0))],
        out_shape=[jax.ShapeDtypeStruct((TOP_K, n), F32), jax.ShapeDtypeStruct((TOP_K, n), I32),
                   table_shape, table_shape, table_shape, list_shape, list_shape, list_shape,
                   jax.ShapeDtypeStruct((N_EXPERTS, LANES), I32)],
        scratch_shapes=[pltpu.VMEM((N_EXPERTS, LANES), F32)],
        compiler_params=pltpu.CompilerParams(dimension_semantics=("arbitrary",), vmem_limit_bytes=VMEM_LIMIT_BYTES),
        name=f"router_l{layer}",
    )(h, w_router_p, b_router_p, upper, lower)


def _for_each_chunk(n_chunks, crow_ref, table_row, chunk_fn):
    base = table_row * LANES

    def body(j, carry):
        chunk_fn(pl.multiple_of(j * PERM_CHUNK, PERM_CHUNK), pl.multiple_of(crow_ref[base + j], PERM_GRAN))
        return carry

    lax.fori_loop(0, n_chunks, body, 0)


def _for_used_rows(used_rows, fn):
    lower = -1
    for size in PERM_LOCAL_SIZES:
        if size == PERM_LOCAL_SIZES[-1]:
            cond = used_rows > lower
        else:
            cond = (used_rows > lower) & (used_rows <= size)
        pl.when(cond)(functools.partial(fn, size))
        lower = size


def _dispatch_kernel(pend_ref, pcnt_ref, cnt_ref, off_ref, before_ref, crow_ref, ctot_ref, h_ref, pos_ref, xs_ref,
                     xc_ref, zbuf_ref, tail_ref, tot_ref, sem, zsem, *, block_rows):
    i = pl.program_id(0)
    slot = i % 2

    def fill(first_row):
        start = pl.multiple_of(first_row, block_rows)
        return pltpu.make_async_copy(zbuf_ref, xs_ref.at[pl.ds(start, block_rows)], zsem)

    @pl.when(i == 0)
    def _():
        zbuf_ref[...] = jnp.zeros_like(zbuf_ref)
        tail_ref[...] = jnp.zeros_like(tail_ref)
        xc_ref[...] = jnp.zeros_like(xc_ref)
        used_blocks = pend_ref[N_EXPERTS - 1] // block_rows
        all_blocks = xs_ref.shape[0] // block_rows
        for e in range(N_EXPERTS):
            @pl.when(pcnt_ref[e] > 0)
            def _():
                fill(pend_ref[e] - block_rows).start()
        lax.fori_loop(used_blocks, all_blocks, lambda j, c: (fill(j * block_rows).start(), c)[1], 0)
        for e in range(N_EXPERTS):
            @pl.when(pcnt_ref[e] > 0)
            def _():
                fill(pend_ref[e] - block_rows).wait()
        lax.fori_loop(used_blocks, all_blocks, lambda j, c: (fill(j * block_rows).wait(), c)[1], 0)

    def sort_rows(n_rows):
        pos = pos_ref[...]
        local_row = lax.broadcasted_iota(I32, (n_rows, pos.shape[1]), 0)
        p = jnp.where(local_row == pos[0:1], 1.0, 0.0)
        for k in range(1, TOP_K):
            p = p + jnp.where(local_row == pos[k:k + 1], 1.0, 0.0)
        pb = p.astype(BF16)
        xc_ref[slot, 0:n_rows, :] = _dot(pb, h_ref[...].astype(BF16)).astype(BF16)

    _for_used_rows(ctot_ref[i % SMEM_TABLE] * PERM_CHUNK, sort_rows)

    def copy(s, local_row, global_row, size):
        return pltpu.make_async_copy(xc_ref.at[s, pl.ds(local_row, size)],
                                     xs_ref.at[pl.ds(global_row, size)], sem.at[s])

    def drain(s):
        @pl.when(tot_ref[0] > 0)
        def _():
            copy(s, 0, 0, tot_ref[0] * PERM_CHUNK).wait()

    @pl.when(i > 0)
    def _():
        drain(1 - slot)

    group_row = lax.broadcasted_iota(I32, (PERM_GRAN, 1), 0)

    def carry_partial_group(e, n, loc, phase):
        first = pl.ds(pl.multiple_of(loc, PERM_GRAN), PERM_GRAN)
        merged = jnp.where((group_row < phase) & (n > 0), tail_ref[e], xc_ref[slot, first, :])
        xc_ref[slot, first, :] = merged
        end = pl.ds(pl.multiple_of(loc + ((phase + n) & -PERM_GRAN), PERM_GRAN), PERM_GRAN)
        tail_ref[e] = jnp.where(n > 0, xc_ref[slot, end, :], tail_ref[e])

    base = (i % TABLE_GROUP) * LANES
    for e in range(N_EXPERTS):
        carry_partial_group(e, cnt_ref[base + e], off_ref[base + e], before_ref[base + e] & (PERM_GRAN - 1))
    tot_ref[0] = ctot_ref[i % SMEM_TABLE]
    _for_each_chunk(tot_ref[0], crow_ref, i % TABLE_GROUP, lambda lr, gr: copy(slot, lr, gr, PERM_CHUNK).start())

    @pl.when(i == pl.num_programs(0) - 1)
    def _():
        drain(slot)


def _table_spec(shift, nblk, per_block=LANES):
    group = SMEM_TABLE // per_block
    return pl.BlockSpec((SMEM_TABLE,), lambda i, *_: (jnp.minimum(i + shift, nblk - 1) // group,),
                        memory_space=pltpu.SMEM)


def _dispatch(h, pos, cnt_tbl, off_tbl, before_tbl, crow_tbl, ctot_tbl, pend, pcnt, total_rows, layer):
    n, d = h.shape
    rows = min(PERM_ROWS, n)
    nblk = n // rows
    return pl.pallas_call(
        functools.partial(_dispatch_kernel, block_rows=EXPERT_ROWS),
        grid_spec=pltpu.PrefetchScalarGridSpec(
            num_scalar_prefetch=2,
            grid=(nblk,),
            in_specs=[_table_spec(0, nblk), _table_spec(0, nblk), _table_spec(0, nblk), _table_spec(0, nblk),
                      _table_spec(0, nblk, per_block=1),
                      pl.BlockSpec((rows, d), lambda i, *_: (i, 0)),
                      pl.BlockSpec((TOP_K, rows), lambda i, *_: (0, i))],
            out_specs=pl.BlockSpec(memory_space=pl.ANY),
            scratch_shapes=[pltpu.VMEM((2, PERM_LOCAL, d), BF16), pltpu.VMEM((EXPERT_ROWS, d), BF16),
                            pltpu.VMEM((N_EXPERTS, PERM_GRAN, d), BF16), pltpu.SMEM((1,), I32),
                            pltpu.SemaphoreType.DMA((2,)), pltpu.SemaphoreType.DMA(())],
        ),
        out_shape=jax.ShapeDtypeStruct((total_rows, d), BF16),
        compiler_params=pltpu.CompilerParams(dimension_semantics=("arbitrary",), vmem_limit_bytes=VMEM_LIMIT_BYTES),
        name=f"dispatch_l{layer}",
    )(pend, pcnt, cnt_tbl, off_tbl, before_tbl, crow_tbl, ctot_tbl, h, pos)


def _combine_kernel(crow_ref, ctot_ref, ncrow_ref, nctot_ref, h_ref, gate_ref, pos_ref,
                    lg_ref, lb_ref, ys_ref, o_ref, yc_ref, sem, *, alpha):
    i = pl.program_id(0)
    slot = i % 2

    def copy(s, local_row, global_row, size):
        return pltpu.make_async_copy(ys_ref.at[pl.ds(global_row, size)],
                                     yc_ref.at[s, pl.ds(local_row, size)], sem.at[s])

    @pl.when(i == 0)
    def _():
        yc_ref[...] = jnp.zeros_like(yc_ref)
        _for_each_chunk(ctot_ref[0], crow_ref, 0, lambda lr, gr: copy(0, lr, gr, PERM_CHUNK).start())

    @pl.when(i + 1 < pl.num_programs(0))
    def _():
        _for_each_chunk(nctot_ref[(i + 1) % SMEM_TABLE], ncrow_ref, (i + 1) % TABLE_GROUP,
                        lambda lr, gr: copy(1 - slot, lr, gr, PERM_CHUNK).start())

    n_chunks = ctot_ref[i % SMEM_TABLE]

    @pl.when(n_chunks > 0)
    def _():
        copy(slot, 0, 0, n_chunks * PERM_CHUNK).wait()

    def unsort_rows(n_rows):
        pos = pos_ref[...]
        gates = gate_ref[...]
        local_row = lax.broadcasted_iota(I32, (n_rows, pos.shape[1]), 0)
        q = jnp.where(local_row == pos[0:1], gates[0:1], 0.0)
        for k in range(1, TOP_K):
            q = q + jnp.where(local_row == pos[k:k + 1], gates[k:k + 1], 0.0)
        qb = q.astype(BF16)
        ffn = _dot_tn(qb, yc_ref[slot, 0:n_rows, :])
        o_ref[...] = _layer_norm(alpha * h_ref[...] + ffn, lg_ref[...], lb_ref[...])

    _for_used_rows(n_chunks * PERM_CHUNK, unsort_rows)


def _combine(h, gates, pos, crow_tbl, ctot_tbl, ys, ln_g, ln_b, layer, alpha):
    n, d = h.shape
    rows = min(PERM_ROWS, n)
    nblk = n // rows
    return pl.pallas_call(
        functools.partial(_combine_kernel, alpha=alpha),
        grid=(nblk,),
        in_specs=[_table_spec(0, nblk), _table_spec(0, nblk, per_block=1),
                  _table_spec(1, nblk), _table_spec(1, nblk, per_block=1),
                  pl.BlockSpec((rows, d), lambda i: (i, 0)),
                  pl.BlockSpec((TOP_K, rows), lambda i: (0, i)),
                  pl.BlockSpec((TOP_K, rows), lambda i: (0, i)),
                  pl.BlockSpec((None, 1, d), lambda i: (layer, 0, 0)),
                  pl.BlockSpec((None, 1, d), lambda i: (layer, 0, 0)),
                  pl.BlockSpec(memory_space=pl.ANY)],
        out_specs=pl.BlockSpec((rows, d), lambda i: (i, 0)),
        out_shape=jax.ShapeDtypeStruct((n, d), F32),
        scratch_shapes=[pltpu.VMEM((2, PERM_LOCAL, d), BF16), pltpu.SemaphoreType.DMA((2,))],
        compiler_params=pltpu.CompilerParams(dimension_semantics=("arbitrary",), vmem_limit_bytes=VMEM_LIMIT_BYTES),
        name=f"combine_l{layer}",
    )(crow_tbl, ctot_tbl, crow_tbl, ctot_tbl, h, gates, pos, ln_g, ln_b, ys)


def _expert_kernel(be_ref, nact_ref, x_ref, wgu_ref, bgu_ref, wd_ref, bd_ref, y_ref, wgu_b_ref, wd_b_ref):
    j = pl.program_id(0)
    d_ff = wd_ref.shape[0]

    @pl.when((j == 0) | (be_ref[j] != be_ref[jnp.maximum(j - 1, 0)]))
    def _():
        wgu_b_ref[...] = wgu_ref[...].astype(BF16)
        wd_b_ref[...] = wd_ref[...].astype(BF16)

    @pl.when(j < nact_ref[0])
    def _():
        gu = _dot(x_ref[...], wgu_b_ref[...]) + bgu_ref[...]
        gate = jnp.minimum(gu[:, :d_ff], SWIGLU_LIMIT)
        up = jnp.clip(gu[:, d_ff:], -SWIGLU_LIMIT, SWIGLU_LIMIT)
        act = (up + 1.0) * gate * _sigmoid(SWIGLU_ALPHA * gate)
        y_ref[...] = (_dot(act.astype(BF16), wd_b_ref[...]) + bd_ref[...]).astype(BF16)

    @pl.when(j >= nact_ref[0])
    def _():
        y_ref[...] = jnp.zeros_like(y_ref)


def _experts(xs, blk_expert, nact, wgu, bgu, wd, bd, layer):
    total_rows, d = xs.shape
    nblk = total_rows // EXPERT_ROWS
    d_ff = wd.shape[2]

    def row_block(j, be, na):
        return (j, 0)

    def of_expert(j, be, na):
        return (layer, be[j], 0, 0)

    return pl.pallas_call(
        _expert_kernel,
        grid_spec=pltpu.PrefetchScalarGridSpec(
            num_scalar_prefetch=2,
            grid=(nblk,),
            in_specs=[pl.BlockSpec((EXPERT_ROWS, d), row_block),
                      pl.BlockSpec((None, None, d, 2 * d_ff), of_expert),
                      pl.BlockSpec((None, None, 1, 2 * d_ff), of_expert),
                      pl.BlockSpec((None, None, d_ff, d), of_expert),
                      pl.BlockSpec((None, None, 1, d), of_expert)],
            out_specs=pl.BlockSpec((EXPERT_ROWS, d), row_block),
            scratch_shapes=[pltpu.VMEM((d, 2 * d_ff), BF16), pltpu.VMEM((d_ff, d), BF16)],
        ),
        out_shape=jax.ShapeDtypeStruct((total_rows, d), BF16),
        compiler_params=pltpu.CompilerParams(dimension_semantics=("arbitrary",), vmem_limit_bytes=VMEM_LIMIT_BYTES),
        name=f"experts_l{layer}",
    )(blk_expert, nact, xs, wgu, bgu, wd, bd)


def _moe(h, w_router_p, b_router_p, wgu, bgu, wd, bd, ln_g, ln_b, layer, alpha):
    n, d = h.shape
    (gates, pos, cnt_tbl, off_tbl, before_tbl, crow_tbl, cexp_tbl, ctot_tbl,
     totals) = _router(h, w_router_p, b_router_p, layer)
    counts = totals[:, 0]
    padded = (counts + PERM_CHUNK + EXPERT_ROWS - 1) // EXPERT_ROWS * EXPERT_ROWS
    padded = jnp.where(counts > 0, padded, 0)
    pend = jnp.cumsum(padded).astype(I32)
    pstart = pend - padded
    nblk = (n * TOP_K + N_EXPERTS * PERM_CHUNK) // EXPERT_ROWS + N_EXPERTS
    nact = (pend[-1] // EXPERT_ROWS).reshape(1).astype(I32)
    blk = jnp.minimum(jnp.arange(nblk, dtype=I32), nact[0] - 1) * EXPERT_ROWS
    blk_expert = jnp.minimum(jnp.sum(pend[None, :] <= blk[:, None], axis=1), N_EXPERTS - 1).astype(I32)

    def flat(tbl):
        tbl = tbl.reshape(-1)
        return jnp.pad(tbl, (0, -tbl.shape[0] % SMEM_TABLE))

    def per_expert(tbl):
        return flat(jnp.pad(tbl[:, :, 0], ((0, 0), (0, LANES - N_EXPERTS))))

    cexp = cexp_tbl[:, 0, :]
    region = jnp.sum(jnp.where(cexp[:, :, None] == jnp.arange(N_EXPERTS)[None, None, :], pstart[None, None, :], 0), axis=-1)
    crow_f = flat(crow_tbl[:, 0, :] + region)
    ctot_f = flat(ctot_tbl[:, 0, 0])

    xs = _dispatch(h, pos, per_expert(cnt_tbl), per_expert(off_tbl), per_expert(before_tbl), crow_f, ctot_f,
                   pend, padded.astype(I32), nblk * EXPERT_ROWS, layer)
    ys = _experts(xs, blk_expert, nact, wgu, bgu, wd, bd, layer)
    return _combine(h, gates, pos, crow_f, ctot_f, ys, ln_g, ln_b, layer, alpha)


def _relayout_w_in(w_in):
    depth, d, _ = w_in.shape
    out = jnp.zeros((depth, d, D_IN_PAD), BF16)
    for g in range(len(IN_SPLITS)):
        lo, hi = _COLS[g]
        out = out.at[:, :, lo:hi].set(w_in[:, :, _SRC_OFF[g]:_SRC_OFF[g] + IN_SPLITS[g]].astype(BF16))
    return out


def kernel(x, ln_in_g, ln_in_b, w_in, w_gla_gate_expand, b_gla_gate, gla_norm_g, hgrn_lb_raw, hgrn_norm_g, conv_w, w_out, ln_mix_g, ln_mix_b, w_router, b_router, w_gate_up, b_gate_up, w_down, b_down, ln_ffn_g, ln_ffn_b):
    bsz, t, d = x.shape
    depth = w_in.shape[0]
    alpha = (2 * depth) ** 0.25
    n = bsz * t

    w_in_r = _relayout_w_in(w_in)
    w_exp = jnp.zeros((depth, LANES, GLA_K_DIM), F32).at[:, :GLA_GATE_RANK, :].set(w_gla_gate_expand)
    w_exp_hi = w_exp.astype(BF16)
    w_exp = jnp.concatenate([w_exp_hi, (w_exp - w_exp_hi.astype(F32)).astype(BF16)], axis=-1)
    b_gate = b_gla_gate.reshape(depth, 1, GLA_K_DIM)
    gla_g = jnp.tile(gla_norm_g, (1, GLA_HEADS)).reshape(depth, 1, GLA_V_DIM)
    hgrn_g = jnp.tile(hgrn_norm_g, (1, HGRN_HEADS)).reshape(depth, 1, HGRN_I_DIM)
    w_out_b = w_out.astype(BF16)
    w_router_t = jnp.swapaxes(w_router, 1, 2)
    w_router_hi = w_router_t.astype(BF16)
    w_router_lo = (w_router_t - w_router_hi.astype(F32)).astype(BF16)
    w_router_p = jnp.concatenate([w_router_hi, w_router_lo], axis=1)
    b_router_p = b_router.reshape(depth, N_EXPERTS, 1)
    wgu, wd = w_gate_up, w_down
    bgu = b_gate_up.reshape(depth, N_EXPERTS, 1, -1)
    bd = b_down.reshape(depth, N_EXPERTS, 1, -1)
    row = lambda a: a.reshape(depth, 1, d)
    consts = _mixer_consts(min(CUMSUM_ROWS, t))

    h = x
    for l in range(depth):
        h = _mixer(h, ln_in_g.reshape(1, d), ln_in_b.reshape(1, d), w_in_r, w_exp, b_gate, gla_g, hgrn_lb_raw,
                   hgrn_g, conv_w, w_out_b, row(ln_mix_g), row(ln_mix_b), consts,
                   layer=l, pre_ln=(l == 0), alpha=alpha)
        h = _moe(h.reshape(n, d), w_router_p, b_router_p, wgu, bgu, wd, bd, row(ln_ffn_g), row(ln_ffn_b),
                 l, alpha).reshape(bsz, t, d)
    return h
```
